```python
import math
import jax, jax.numpy as jnp
from jax import lax
import numpy as np

D_MODEL = 1024
BATCH = 8
SEQ = 2048
DEPTH = 4
DEC_BATCH = 32
DEC_SEQ = 1
PAST_LEN = 16384
PAGE_SIZE = 128

MLA_HEADS = 8
Q_RANK = 384
KV_RANK = 256
NOPE_DIM = 64
ROPE_DIM = 32
V_DIM = 64
MLA_WIDTH = MLA_HEADS * V_DIM
ROPE_THETA = 10000.0
Q_BLOCK = 128
GMLP_GROUPS = 8
GMLP_CHUNK = 128
GMLP_WIDTH = 512
GMLP_GDIM = GMLP_WIDTH // GMLP_GROUPS
GLA_HEADS = 4
GLA_DK = 64
GLA_DV = 128
GLA_KW = GLA_HEADS * GLA_DK
GLA_VW = GLA_HEADS * GLA_DV
GLA_GATE_RANK = 16
GLA_TAU = 16.0
GLA_CHUNK = 64
N_BRANCH = 3
D_FF = ((8 * D_MODEL // 3 + 255) // 256) * 256
IN_SPLITS = (Q_RANK, KV_RANK, ROPE_DIM, GMLP_WIDTH, GMLP_WIDTH, GLA_KW, GLA_KW, GLA_VW,
             GLA_GATE_RANK, GLA_VW, N_BRANCH * D_MODEL)
D_IN = sum(IN_SPLITS)

kernel_name = 'hybrid_mla_gmlp_gla_decoder_step'


def rmsnorm(x, g, eps=1e-6):
    xf = x.astype(jnp.float32)
    y = xf * lax.rsqrt(jnp.mean(xf * xf, axis=-1, keepdims=True) + eps)
    return (y * g.astype(jnp.float32)).astype(x.dtype)


def layernorm(x, g, b, eps=1e-5):
    xf = x.astype(jnp.float32)
    mu = jnp.mean(xf, axis=-1, keepdims=True)
    xc = xf - mu
    var = jnp.mean(xc * xc, axis=-1, keepdims=True)
    return (xc * lax.rsqrt(var + eps) * g.astype(jnp.float32) + b.astype(jnp.float32)).astype(x.dtype)


def rope(x, pos):
    half = ROPE_DIM // 2
    inv = jnp.power(ROPE_THETA, -jnp.arange(half, dtype=jnp.float32) * 2.0 / ROPE_DIM)
    ang = pos[:, None] * inv[None, :]
    cos = jnp.cos(ang)[None, :, None, :]
    sin = jnp.sin(ang)[None, :, None, :]
    xf = x.astype(jnp.float32)
    x1, x2 = xf[..., :half], xf[..., half:]
    return jnp.concatenate([x1 * cos - x2 * sin, x2 * cos + x1 * sin], axis=-1).astype(x.dtype)


def split_in(z):
    idx = np.cumsum(IN_SPLITS)[:-1].tolist()
    return jnp.split(z, idx, axis=-1)


def mixer_front(x, pos, p):
    b, t = x.shape[:2]
    h = rmsnorm(x, p['g_mix'])
    cq, ckv, kr, u, v, gq, gk, gv, ga, gg, gate = split_in(h @ p['w_in'])
    cq = rmsnorm(cq, p['g_q'])
    q = jnp.einsum('btr,rhd->bthd', cq, p['w_uq'])
    q_nope = q[..., :NOPE_DIM]
    q_rope = rope(q[..., NOPE_DIM:], pos)
    ckv = rmsnorm(ckv, p['g_kv'])
    kr = rope(kr[:, :, None, :], pos)[:, :, 0, :]
    gq = gq.reshape(b, t, GLA_HEADS, GLA_DK) * (GLA_DK ** -0.5)
    gk = gk.reshape(b, t, GLA_HEADS, GLA_DK)
    gv = gv.reshape(b, t, GLA_HEADS, GLA_DV)
    a = ga @ p['w_a2'] + p['b_a']
    logf = (jax.nn.log_sigmoid(a.astype(jnp.float32)) / GLA_TAU).reshape(b, t, GLA_HEADS, GLA_DK)
    return dict(q_nope=q_nope, q_rope=q_rope, ckv=ckv, kr=kr, u=u, v=v,
                gq=gq, gk=gk, gv=gv, logf=logf, gg=gg, gate=gate)


def mla_prompt_attn(q_nope, q_rope, ckv, kr, w_uk, w_uv):
    b, s = q_nope.shape[:2]
    nb = s // Q_BLOCK
    k_nope = jnp.einsum('bsr,rhd->bshd', ckv, w_uk)
    v = jnp.einsum('bsr,rhd->bshd', ckv, w_uv)
    scale = (NOPE_DIM + ROPE_DIM) ** -0.5
    qn = q_nope.reshape(b, nb, Q_BLOCK, MLA_HEADS, NOPE_DIM).swapaxes(0, 1)
    qr = q_rope.reshape(b, nb, Q_BLOCK, MLA_HEADS, ROPE_DIM).swapaxes(0, 1)
    kpos = jnp.arange(s)

    def block(args):
        i, qn_b, qr_b = args
        sc = (jnp.einsum('bqhd,bshd->bhqs', qn_b, k_nope)
              + jnp.einsum('bqhd,bsd->bhqs', qr_b, kr)).astype(jnp.float32) * scale
        qpos = i * Q_BLOCK + jnp.arange(Q_BLOCK)
        mask = kpos[None, :] <= qpos[:, None]
        sc = jnp.where(mask[None, None], sc, -jnp.inf)
        pr = jax.nn.softmax(sc, axis=-1).astype(v.dtype)
        return jnp.einsum('bhqs,bshd->bqhd', pr, v)

    o = lax.map(block, (jnp.arange(nb), qn, qr))
    return o.swapaxes(0, 1).reshape(b, s, MLA_WIDTH)


def mla_sample_attn(q_nope, q_rope, ckv_new, kr_new, ckv_past, kr_past, w_uk, w_uv):
    b, t = q_nope.shape[:2]
    npast = ckv_past.shape[1]
    scale = (NOPE_DIM + ROPE_DIM) ** -0.5
    q_lat = jnp.einsum('bthd,rhd->bthr', q_nope, w_uk)
    s_past = (jnp.einsum('bthr,bsr->bhts', q_lat, ckv_past)
              + jnp.einsum('bthd,bsd->bhts', q_rope, kr_past))
    s_new = (jnp.einsum('bthr,bsr->bhts', q_lat, ckv_new)
             + jnp.einsum('bthd,bsd->bhts', q_rope, kr_new))
    sc = jnp.concatenate([s_past, s_new], axis=-1).astype(jnp.float32) * scale
    causal = jnp.arange(t)[None, :] <= jnp.arange(t)[:, None]
    mask = jnp.concatenate([jnp.ones((t, npast), bool), causal], axis=1)
    sc = jnp.where(mask[None, None], sc, -jnp.inf)
    pr = jax.nn.softmax(sc, axis=-1).astype(ckv_past.dtype)
    o_lat = (jnp.einsum('bhts,bsr->bthr', pr[..., :npast], ckv_past)
             + jnp.einsum('bhts,bsr->bthr', pr[..., npast:], ckv_new))
    o = jnp.einsum('bthr,rhd->bthd', o_lat, w_uv)
    return o.reshape(b, t, MLA_WIDTH)


def gmlp_mix(u, v, p):
    v = layernorm(v, p['g_v'], p['b_v'])
    b, t = v.shape[:2]
    c = min(t, GMLP_CHUNK)
    n = t // c
    vg = v.reshape(b, n, c, GMLP_GROUPS, GMLP_GDIM)
    tri = jnp.tril(jnp.ones((c, c), bool))
    ws = jnp.where(tri[None], p['w_s'][:, :c, :c], 0.0).astype(v.dtype)
    mixed = jnp.einsum('gts,bnsgd->bntgd', ws, vg) + p['b_s'][:, :c].T[None, None, :, :, None]
    return u * mixed.reshape(b, t, GMLP_WIDTH), v


def gla_chunked(q, k, v, logf):
    b, t = q.shape[:2]
    c = GLA_CHUNK
    n = t // c

    def to_chunks(z):
        return z.astype(jnp.float32).reshape(b, n, c, *z.shape[2:]).swapaxes(0, 1)

    tri = jnp.tril(jnp.ones((c, c), bool))[None, :, :, None, None]

    def step(S, inp):
        qi, ki, vi, fi = inp
        bcum = jnp.cumsum(fi, axis=1)
        diff = jnp.where(tri, bcum[:, :, None] - bcum[:, None, :], -jnp.inf)
        A = jnp.sum(qi[:, :, None] * ki[:, None] * jnp.exp(diff), axis=-1)
        o = (jnp.einsum('btsh,bshv->bthv', A, vi)
             + jnp.einsum('bthk,bhkv->bthv', qi * jnp.exp(bcum), S))
        blast = bcum[:, -1]
        S = (jnp.exp(blast)[..., None] * S
             + jnp.einsum('bshk,bshv->bhkv', ki * jnp.exp(blast[:, None] - bcum), vi))
        return S, o

    S0 = jnp.zeros((b, GLA_HEADS, GLA_DK, GLA_DV), jnp.float32)
    S, o = lax.scan(step, S0, (to_chunks(q), to_chunks(k), to_chunks(v), to_chunks(logf)))
    return o.swapaxes(0, 1).reshape(b, t, GLA_HEADS, GLA_DV).astype(v.dtype), S


def gla_recurrent(q, k, v, logf, S0):
    def step(S, inp):
        qt, kt, vt, ft = inp
        S = jnp.exp(ft)[..., None] * S + kt[..., :, None] * vt[..., None, :]
        return S, jnp.einsum('bhk,bhkv->bhv', qt, S)

    xs = tuple(z.astype(jnp.float32).swapaxes(0, 1) for z in (q, k, v, logf))
    S, o = lax.scan(step, S0.astype(jnp.float32), xs)
    return o.swapaxes(0, 1).astype(v.dtype), S


def gla_out(o, gg, g_gla):
    b, t = o.shape[:2]
    return rmsnorm(o, g_gla).reshape(b, t, GLA_VW) * jax.nn.silu(gg)


def merge_and_ffn(x, h_a, h_b, h_c, gate, p):
    b, t = x.shape[:2]
    gates = jax.nn.sigmoid(gate.astype(jnp.float32)).astype(x.dtype).reshape(b, t, N_BRANCH, D_MODEL)
    mix = (gates[:, :, 0] * (h_a @ p['w_pa'])
           + gates[:, :, 1] * (h_b @ p['w_pb'])
           + gates[:, :, 2] * (h_c @ p['w_pc']))
    x = x + mix @ p['w_o']
    gt, up = jnp.split(rmsnorm(x, p['g_ffn']) @ p['w_gu'], 2, axis=-1)
    return x + (jax.nn.silu(gt) * up) @ p['w_down']


def setup_inputs(seed: int = 0) -> dict:
    key = jax.random.key(seed)
    ks = jax.random.split(key, 32)
    f32 = jnp.float32
    n_pages = PAST_LEN // PAGE_SIZE
    n_used = DEC_BATCH * n_pages
    n_pool = (5 * n_used + 3) // 4

    def nrm(k, shape, scale):
        return jax.random.normal(k, shape, f32) * scale

    def gain(k, shape):
        return 1.0 + 0.05 * jax.random.normal(k, shape, f32)

    page_table = jax.random.permutation(ks[5], n_pool)[:n_used].reshape(DEC_BATCH, n_pages).astype(jnp.int32)
    return {
        'x_prompt': nrm(ks[0], (BATCH, SEQ, D_MODEL), 1.0),
        'x_sample': nrm(ks[1], (DEC_BATCH, DEC_SEQ, D_MODEL), 1.0),
        'cache_ckv': nrm(ks[2], (DEPTH, n_pool, PAGE_SIZE, KV_RANK), 1.0),
        'cache_krope': nrm(ks[3], (DEPTH, n_pool, PAGE_SIZE, ROPE_DIM), 1.0),
        'state_gla': nrm(ks[4], (DEPTH, DEC_BATCH, GLA_HEADS, GLA_DK, GLA_DV), 1.0),
        'page_table': page_table,
        'g_mix': gain(ks[6], (DEPTH, D_MODEL)),
        'w_in': nrm(ks[7], (DEPTH, D_MODEL, D_IN), D_MODEL ** -0.5),
        'g_q': gain(ks[8], (DEPTH, Q_RANK)),
        'w_uq': nrm(ks[9], (DEPTH, Q_RANK, MLA_HEADS, NOPE_DIM + ROPE_DIM), Q_RANK ** -0.5),
        'g_kv': gain(ks[10], (DEPTH, KV_RANK)),
        'w_uk': nrm(ks[11], (DEPTH, KV_RANK, MLA_HEADS, NOPE_DIM), KV_RANK ** -0.5),
        'w_uv': nrm(ks[12], (DEPTH, KV_RANK, MLA_HEADS, V_DIM), KV_RANK ** -0.5),
        'g_v': gain(ks[13], (DEPTH, GMLP_WIDTH)),
        'b_v': nrm(ks[14], (DEPTH, GMLP_WIDTH), 0.02),
        'w_s': nrm(ks[15], (DEPTH, GMLP_GROUPS, GMLP_CHUNK, GMLP_CHUNK), GMLP_CHUNK ** -0.5),
        'b_s': 1.0 + nrm(ks[16], (DEPTH, GMLP_GROUPS, GMLP_CHUNK), 0.1),
        'w_a2': nrm(ks[17], (DEPTH, GLA_GATE_RANK, GLA_KW), GLA_GATE_RANK ** -0.5),
        'b_a': nrm(ks[18], (DEPTH, GLA_KW), 0.1),
        'g_gla': gain(ks[19], (DEPTH, GLA_DV)),
        'w_pa': nrm(ks[20], (DEPTH, MLA_WIDTH, D_MODEL), MLA_WIDTH ** -0.5),
        'w_pb': nrm(ks[21], (DEPTH, GMLP_WIDTH, D_MODEL), GMLP_WIDTH ** -0.5),
        'w_pc': nrm(ks[22], (DEPTH, GLA_VW, D_MODEL), GLA_VW ** -0.5),
        'w_o': nrm(ks[23], (DEPTH, D_MODEL, D_MODEL), D_MODEL ** -0.5),
        'g_ffn': gain(ks[24], (DEPTH, D_MODEL)),
        'w_gu': nrm(ks[25], (DEPTH, D_MODEL, 2 * D_FF), D_MODEL ** -0.5),
        'w_down': nrm(ks[26], (DEPTH, D_FF, D_MODEL), D_FF ** -0.5),
        'g_final': gain(ks[27], (D_MODEL,)),
    }


def reference(x_prompt, x_sample, cache_ckv, cache_krope, state_gla, page_table,
              g_mix, w_in, g_q, w_uq, g_kv, w_uk, w_uv, g_v, b_v, w_s, b_s,
              w_a2, b_a, g_gla, w_pa, w_pb, w_pc, w_o, g_ffn, w_gu, w_down, g_final):
    t_p = x_prompt.shape[1]
    t_s = x_sample.shape[1]
    b_s_dec = x_sample.shape[0]
    past = page_table.shape[1] * PAGE_SIZE
    pos_p = jnp.arange(t_p, dtype=jnp.float32)
    pos_s = past + jnp.arange(t_s, dtype=jnp.float32)
    xp, xs = x_prompt, x_sample
    ckv_p, kr_p, gla_p, ckv_s, kr_s, gla_s, gv_s = [], [], [], [], [], [], []
    for l in range(DEPTH):
        p = dict(g_mix=g_mix[l], w_in=w_in[l], g_q=g_q[l], w_uq=w_uq[l], g_kv=g_kv[l],
                 w_uk=w_uk[l], w_uv=w_uv[l], g_v=g_v[l], b_v=b_v[l], w_s=w_s[l], b_s=b_s[l],
                 w_a2=w_a2[l], b_a=b_a[l], g_gla=g_gla[l], w_pa=w_pa[l], w_pb=w_pb[l],
                 w_pc=w_pc[l], w_o=w_o[l], g_ffn=g_ffn[l], w_gu=w_gu[l], w_down=w_down[l])
        f = mixer_front(xp, pos_p, p)
        h_a = mla_prompt_attn(f['q_nope'], f['q_rope'], f['ckv'], f['kr'], p['w_uk'], p['w_uv'])
        h_b, _ = gmlp_mix(f['u'], f['v'], p)
        o_c, s_c = gla_chunked(f['gq'], f['gk'], f['gv'], f['logf'])
        h_c = gla_out(o_c, f['gg'], p['g_gla'])
        xp = merge_and_ffn(xp, h_a, h_b, h_c, f['gate'], p)
        ckv_p.append(f['ckv'])
        kr_p.append(f['kr'])
        gla_p.append(s_c.astype(state_gla.dtype))
        f = mixer_front(xs, pos_s, p)
        ckv_past = cache_ckv[l, page_table].reshape(b_s_dec, past, KV_RANK)
        kr_past = cache_krope[l, page_table].reshape(b_s_dec, past, ROPE_DIM)
        h_a = mla_sample_attn(f['q_nope'], f['q_rope'], f['ckv'], f['kr'], ckv_past, kr_past,
                              p['w_uk'], p['w_uv'])
        h_b, v_new = gmlp_mix(f['u'], f['v'], p)
        o_c, s_c = gla_recurrent(f['gq'], f['gk'], f['gv'], f['logf'], state_gla[l])
        h_c = gla_out(o_c, f['gg'], p['g_gla'])
        xs = merge_and_ffn(xs, h_a, h_b, h_c, f['gate'], p)
        ckv_s.append(f['ckv'])
        kr_s.append(f['kr'])
        gla_s.append(s_c.astype(state_gla.dtype))
        gv_s.append(v_new)
    y_prompt = rmsnorm(xp, g_final)
    y_sample = rmsnorm(xs, g_final)
    return (y_prompt, y_sample, jnp.stack(ckv_p), jnp.stack(kr_p), jnp.stack(gla_p),
            jnp.stack(ckv_s), jnp.stack(kr_s), jnp.stack(gla_s), jnp.stack(gv_s))
```

```python
import functools
import math

import numpy as np
import jax
import jax.numpy as jnp
from jax import lax
from jax.experimental import pallas as pl
from jax.experimental.pallas import tpu as pltpu

D_MODEL = 1024
PAGE_SIZE = 128
MLA_HEADS = 8
Q_RANK = 384
KV_RANK = 256
NOPE_DIM = 64
ROPE_DIM = 32
V_DIM = 64
ROPE_THETA = 10000.0
GMLP_GROUPS = 8
GMLP_CHUNK = 128
GMLP_WIDTH = 512
GLA_HEADS = 4
GLA_DK = 64
GLA_DV = 128
GLA_KW = GLA_HEADS * GLA_DK
GLA_VW = GLA_HEADS * GLA_DV
GLA_GATE_RANK = 16
GLA_TAU = 16.0
N_BRANCH = 3
D_FF = 2816
IN_SPLITS = (Q_RANK, KV_RANK, ROPE_DIM, GMLP_WIDTH, GMLP_WIDTH, GLA_KW, GLA_KW, GLA_VW,
             GLA_GATE_RANK, GLA_VW, N_BRANCH * D_MODEL)

LANE = 128
HEAD_PAD = 128
GLA_CHUNK = 128
FF_CHUNK = 256
VMEM_LIMIT = 56 * 1024 * 1024

BF = jnp.bfloat16
F32 = jnp.float32


def _dot(a, b):
    return jnp.dot(a, b, preferred_element_type=F32)


def _dot_nt(a, b):
    return lax.dot_general(a, b, (((1,), (1,)), ((), ())), preferred_element_type=F32)


def _dot_tn(a, b):
    return lax.dot_general(a, b, (((0,), (0,)), ((), ())), preferred_element_type=F32)


def _rms(x, g, eps=1e-6):
    return x * lax.rsqrt(jnp.mean(x * x, axis=-1, keepdims=True) + eps) * g


def _const_spec(shape):
    nd = len(shape)
    return pl.BlockSpec(shape, lambda *_: (0,) * nd, pipeline_mode=pl.Buffered(1))


def _params(sem):
    return pltpu.CompilerParams(dimension_semantics=sem, vmem_limit_bytes=VMEM_LIMIT)


def _prep_layer(l, g_mix, w_in, g_q, w_uq, g_kv, w_uk, w_uv, g_v, b_v, w_s, b_s, w_a2, b_a, g_gla,
                w_pa, w_pb, w_pc, w_o, g_ffn, w_gu, w_down):
    o = np.cumsum((0,) + IN_SPLITS)
    wi = w_in[l]
    cols = lambda i: wi[:, o[i]:o[i + 1]]
    half = ROPE_DIM // 2
    kr = cols(2)
    kr_swap = jnp.concatenate([-kr[:, half:], kr[:, :half]], axis=1)
    z16 = jnp.zeros((D_MODEL, 16), F32)
    z32 = jnp.zeros((D_MODEL, 32), F32)
    small = jnp.concatenate([kr_swap, cols(8), z16, kr, z32], axis=1)
    wf = jnp.concatenate([cols(0), cols(1), small, cols(3), cols(4), cols(5), cols(6), cols(7), cols(9)],
                         axis=1).astype(BF)
    wgate = cols(10).astype(BF)

    uq = w_uq[l]
    qn, qr = uq[..., :NOPE_DIM], uq[..., NOPE_DIM:]
    qr_swap = jnp.concatenate([-qr[..., half:], qr[..., :half]], axis=-1)
    zq = jnp.zeros((Q_RANK, MLA_HEADS, HEAD_PAD - NOPE_DIM - ROPE_DIM), F32)
    wq_cat = jnp.concatenate([qn, qr, zq], axis=-1).reshape(Q_RANK, MLA_HEADS * HEAD_PAD).astype(BF)
    wq_swap = jnp.concatenate([jnp.zeros_like(qn), qr_swap, zq], axis=-1).reshape(Q_RANK, MLA_HEADS * HEAD_PAD).astype(BF)

    uk = w_uk[l]
    wk_pad = jnp.concatenate([uk, jnp.zeros_like(uk)], axis=-1).reshape(KV_RANK, MLA_HEADS * HEAD_PAD).astype(BF)
    uv = w_uv[l]
    zv = jnp.zeros_like(uv)
    even = (jnp.arange(MLA_HEADS) % 2 == 0)[None, :, None]
    wv_pad = jnp.concatenate([jnp.where(even, uv, zv), jnp.where(even, zv, uv)], axis=-1)
    wv_pad = wv_pad.reshape(KV_RANK, MLA_HEADS * HEAD_PAD).astype(BF)
    wukt = jnp.concatenate([uk.transpose(1, 2, 0), jnp.zeros((MLA_HEADS, HEAD_PAD - NOPE_DIM, KV_RANK), F32)],
                           axis=1).astype(BF)
    eye_h = jnp.eye(MLA_HEADS, dtype=F32)
    wuv_wide = jnp.einsum('rhd,hg->hrgd', uv, eye_h).reshape(MLA_HEADS, KV_RANK, MLA_HEADS * V_DIM).astype(BF)

    wa2_pad = jnp.zeros((LANE, GLA_KW), F32).at[32:32 + GLA_GATE_RANK].set(w_a2[l]).astype(BF)
    tri = jnp.tril(jnp.ones((GMLP_CHUNK, GMLP_CHUNK), bool))
    ws = jnp.where(tri[None], w_s[l], 0.0).astype(BF)
    gd = GMLP_WIDTH // GMLP_GROUPS
    bias_tab = jnp.repeat(b_s[l].T, gd, axis=1)
    ws00 = jnp.repeat(w_s[l][:, 0, 0], gd)[None, :]
    bs0 = jnp.repeat(b_s[l][:, 0], gd)[None, :]

    nff = D_FF // FF_CHUNK
    wg = w_gu[l][:, :D_FF].reshape(D_MODEL, nff, FF_CHUNK).transpose(1, 0, 2).astype(BF)
    wu = w_gu[l][:, D_FF:].reshape(D_MODEL, nff, FF_CHUNK).transpose(1, 0, 2).astype(BF)
    wd = w_down[l].reshape(nff, FF_CHUNK, D_MODEL).astype(BF)
    row = lambda a: a.reshape(1, -1).astype(F32)
    return dict(
        wf=wf, wgate=wgate, wq_cat=wq_cat, wq_swap=wq_swap, wk_pad=wk_pad, wv_pad=wv_pad, wukt=wukt,
        wuv_wide=wuv_wide, wa2_pad=wa2_pad, ws=ws, bias_tab=bias_tab, ws00=ws00, bs0=bs0,
        g_mix=row(g_mix[l]), g_q=row(g_q[l]), g_kv=row(g_kv[l]), g_v=row(g_v[l]), b_v=row(b_v[l]),
        b_a=row(b_a[l]), g_gla=row(g_gla[l]), g_ffn=row(g_ffn[l]),
        wpa=w_pa[l].astype(BF), wpb=w_pb[l].astype(BF), wpc=w_pc[l].astype(BF), wo=w_o[l].astype(BF),
        wg=wg, wu=wu, wd=wd)


def _rope_tables(pos):
    half = ROPE_DIM // 2
    inv = jnp.power(ROPE_THETA, -jnp.arange(half, dtype=F32) * 2.0 / ROPE_DIM)
    ang = pos[:, None] * inv[None, :]
    cos, sin = jnp.cos(ang), jnp.sin(ang)
    t = pos.shape[0]
    ctab = jnp.concatenate([jnp.ones((t, NOPE_DIM), F32), cos, cos, jnp.zeros((t, 32), F32)], axis=1)
    stab = jnp.concatenate([jnp.zeros((t, NOPE_DIM), F32), sin, sin, jnp.zeros((t, 32), F32)], axis=1)
    return ctab, stab


_WF_OFF = np.cumsum((0, Q_RANK, KV_RANK, LANE, GMLP_WIDTH, GMLP_WIDTH, GLA_KW, GLA_KW, GLA_VW, GLA_VW))


def _front_kernel(prompt, tm, x_ref, ct_ref, st_ref, wf_ref, wq_cat_ref, wq_swap_ref, wkv_a_ref, wkv_b_ref,
                  wa2_ref, gm_a_ref, gm_b_ref, gmix_ref, gq_ref, gkv_ref, gv_ref, bv_ref, ba_ref, *outs):
    if prompt:
        (q_ref, k_ref, v_ref, ckv_ref, kr_ref, hb_ref, oq_ref, ok_ref, ov_ref, lf_ref, sg_ref) = outs
    else:
        (q_ref, ql_ref, ckv_ref, kr_ref, hb_ref, vn_ref, oq_ref, ok_ref, ov_ref, lf_ref, sg_ref) = outs
    wcol = lambda i: wf_ref[:, _WF_OFF[i]:_WF_OFF[i + 1]]
    x = x_ref[...]
    h = _rms(x, gmix_ref[...]).astype(BF)
    ct = ct_ref[...]
    st = st_ref[...]
    scale = (NOPE_DIM + ROPE_DIM) ** -0.5

    cq = _rms(_dot(h, wcol(0)), gq_ref[...]).astype(BF)
    qa = _dot(cq, wq_cat_ref[...])
    qb = _dot(cq, wq_swap_ref[...])
    for hd in range(MLA_HEADS):
        sl = slice(hd * HEAD_PAD, (hd + 1) * HEAD_PAD)
        qh = (qa[:, sl] * ct + qb[:, sl] * st) * scale
        q_ref[:, sl] = qh.astype(q_ref.dtype)
        if not prompt:
            ql_ref[:, hd * KV_RANK:(hd + 1) * KV_RANK] = _dot(qh.astype(BF), wkv_a_ref[hd])

    ckv = _rms(_dot(h, wcol(1)), gkv_ref[...])
    ckv_ref[...] = ckv
    small = _dot(h, wcol(2))
    lane = lax.broadcasted_iota(jnp.int32, small.shape, 1)
    kr_rot = jnp.where((lane >= NOPE_DIM) & (lane < NOPE_DIM + ROPE_DIM),
                       small * ct + pltpu.roll(small, NOPE_DIM, 1) * st, 0.0)
    kr_ref[...] = pltpu.roll(kr_rot, NOPE_DIM, 1)[:, :ROPE_DIM]
    if prompt:
        ckvb = ckv.astype(BF)
        kn = _dot(ckvb, wkv_a_ref[...])
        for hd in range(MLA_HEADS):
            sl = slice(hd * HEAD_PAD, (hd + 1) * HEAD_PAD)
            k_ref[:, sl] = (kn[:, sl] + kr_rot).astype(BF)
        v_ref[...] = _dot(ckvb, wkv_b_ref[...]).astype(BF)

    u = _dot(h, wcol(3))
    v = _dot(h, wcol(4))
    mu = jnp.mean(v, axis=-1, keepdims=True)
    vc = v - mu
    var = jnp.mean(vc * vc, axis=-1, keepdims=True)
    vn = vc * lax.rsqrt(var + 1e-5) * gv_ref[...] + bv_ref[...]
    if prompt:
        vnb = vn.astype(BF)
        lane2 = lax.broadcasted_iota(jnp.int32, (GMLP_CHUNK, LANE), 1)
        gd = GMLP_WIDTH // GMLP_GROUPS
        for c in range(tm // GMLP_CHUNK):
            rs = slice(c * GMLP_CHUNK, (c + 1) * GMLP_CHUNK)
            for gp in range(GMLP_GROUPS // 2):
                cs = slice(gp * LANE, (gp + 1) * LANE)
                vp = vnb[rs, cs]
                mixed = jnp.where(lane2 < gd, _dot(gm_a_ref[2 * gp], vp), _dot(gm_a_ref[2 * gp + 1], vp))
                hb_ref[rs, cs] = (u[rs, cs] * (mixed + gm_b_ref[:, cs])).astype(hb_ref.dtype)
    else:
        vn_ref[...] = vn
        hb_ref[...] = (u * (gm_a_ref[...] * vn + gm_b_ref[...])).astype(hb_ref.dtype)

    oq_ref[...] = _dot(h, wcol(5)) * (GLA_DK ** -0.5)
    ok_ref[...] = _dot(h, wcol(6))
    ov_ref[...] = _dot(h, wcol(7)).astype(ov_ref.dtype)
    a = _dot(small.astype(BF), wa2_ref[...]) + ba_ref[...]
    lf_ref[...] = (jnp.minimum(a, 0.0) - jnp.log(1.0 + jnp.exp(-jnp.abs(a)))) * (1.0 / GLA_TAU)
    gg = _dot(h, wcol(8))
    sg_ref[...] = (gg * jax.nn.sigmoid(gg)).astype(sg_ref.dtype)


def _front(x, ctab, stab, w, prompt, seq):
    r = x.shape[0]
    tm = 512 if prompt else r
    nt = r // tm
    row = lambda n: pl.BlockSpec((tm, n), lambda i: (i, 0))
    if prompt:
        tab = pl.BlockSpec((tm, LANE), lambda i: (i % (seq // tm), 0))
        wkv_a, wkv_b = w['wk_pad'], w['wv_pad']
        gm_a, gm_b = w['ws'], w['bias_tab']
        out_dims = ((1024, BF), (1024, BF), (1024, BF), (KV_RANK, F32), (ROPE_DIM, F32), (GMLP_WIDTH, BF),
                    (GLA_KW, F32), (GLA_KW, F32), (GLA_VW, BF), (GLA_KW, F32), (GLA_VW, BF))
    else:
        tab = pl.BlockSpec((tm, LANE), lambda i: (0, 0))
        wkv_a, wkv_b = w['wukt'], w['wv_pad']
        gm_a, gm_b = w['ws00'], w['bs0']
        out_dims = ((1024, F32), (MLA_HEADS * KV_RANK, F32), (KV_RANK, F32), (ROPE_DIM, F32), (GMLP_WIDTH, BF),
                    (GMLP_WIDTH, F32), (GLA_KW, F32), (GLA_KW, F32), (GLA_VW, F32), (GLA_KW, F32), (GLA_VW, F32))
    consts = (w['wf'], w['wq_cat'], w['wq_swap'], wkv_a, wkv_b, w['wa2_pad'], gm_a, gm_b,
              w['g_mix'], w['g_q'], w['g_kv'], w['g_v'], w['b_v'], w['b_a'])
    return pl.pallas_call(
        functools.partial(_front_kernel, prompt, tm),
        grid=(nt,),
        in_specs=[row(D_MODEL), tab, tab] + [_const_spec(c.shape) for c in consts],
        out_specs=[row(n) for n, _ in out_dims],
        out_shape=[jax.ShapeDtypeStruct((r, n), dt) for n, dt in out_dims],
        compiler_params=_params(("parallel",)),
        name="front_prompt" if prompt else "front_sample",
    )(x, ctab, stab, *consts)


def _attn_kernel(tq, q_ref, k_ref, v_ref, o_ref):
    qi = pl.program_id(2)
    out = jnp.zeros((tq, HEAD_PAD), F32)
    row = lax.broadcasted_iota(jnp.int32, (tq, tq), 0)
    col = lax.broadcasted_iota(jnp.int32, (tq, tq), 1)
    for hh in range(2):
        cs = slice(hh * HEAD_PAD, (hh + 1) * HEAD_PAD)
        q = q_ref[:, cs]

        def step(kb, vb, carry, mask):
            m, l, acc = carry
            s = _dot_nt(q, kb)
            if mask:
                s = jnp.where(col <= row, s, -jnp.inf)
            m_new = jnp.maximum(m, jnp.max(s, axis=-1, keepdims=True))
            alpha = jnp.exp(m - m_new)
            p = jnp.exp(s - m_new)
            l = alpha * l + jnp.sum(p, axis=-1, keepdims=True)
            acc = alpha * acc + _dot(p.astype(BF), vb)
            return m_new, l, acc

        def body(j, carry):
            off = pl.multiple_of(j * tq, tq)
            return step(k_ref[pl.ds(off, tq), cs], v_ref[pl.ds(off, tq), cs], carry, False)

        init = (jnp.full((tq, 1), -jnp.inf, F32), jnp.zeros((tq, 1), F32), jnp.zeros((tq, HEAD_PAD), F32))
        carry = lax.fori_loop(0, qi, body, init)
        off = pl.multiple_of(qi * tq, tq)
        _, l, acc = step(k_ref[pl.ds(off, tq), cs], v_ref[pl.ds(off, tq), cs], carry, True)
        out = out + acc / l
    o_ref[...] = out.astype(o_ref.dtype)


def _attention(q, k, v, batch, seq):
    tq = 256
    npair = MLA_HEADS // 2
    q3, k3, v3 = (a.reshape(batch, seq, MLA_HEADS * HEAD_PAD) for a in (q, k, v))
    out = pl.pallas_call(
        functools.partial(_attn_kernel, tq),
        grid=(batch, npair, seq // tq),
        in_specs=[pl.BlockSpec((None, tq, 2 * HEAD_PAD), lambda b, p, i: (b, i, p)),
                  pl.BlockSpec((None, seq, 2 * HEAD_PAD), lambda b, p, i: (b, 0, p)),
                  pl.BlockSpec((None, seq, 2 * HEAD_PAD), lambda b, p, i: (b, 0, p))],
        out_specs=pl.BlockSpec((None, tq, HEAD_PAD), lambda b, p, i: (b, i, p)),
        out_shape=jax.ShapeDtypeStruct((batch, seq, MLA_HEADS * V_DIM), BF),
        compiler_params=_params(("parallel", "parallel", "arbitrary")),
        name="mla_prompt_attn",
    )(q3, k3, v3)
    return out.reshape(batch * seq, MLA_HEADS * V_DIM)


_GLA_LEVELS = (1, 2, 4, 8, 16, 32, 64)


def _gla_tables():
    c = GLA_CHUNK
    t = np.arange(c)[:, None]
    r = np.arange(c)[None, :]
    blocks = []
    for m in _GLA_LEVELS[1:] + (c,):
        blocks.append(((t // m == r // m) & (r <= t)))
    for m in _GLA_LEVELS[1:] + (c,):
        blocks.append(((t // m == r // m) & (r > t)))
    lstack = np.concatenate(blocks, axis=0).astype(np.float32)
    masks = [np.eye(c, dtype=np.float32)]
    for m in _GLA_LEVELS:
        masks.append((((t // m) % 2 == 1) & (r // m == t // m - 1)).astype(np.float32))
    return jnp.asarray(lstack, BF), jnp.asarray(np.stack(masks), F32)


def _gla_kernel(nchunk, q_ref, k_ref, v_ref, lf_ref, sg_ref, l_ref, mask_ref, g_ref, hc_ref, s_ref,
                px_ref, qt_ref, kt_ref, z_ref):
    c = GLA_CHUNK
    ci = pl.program_id(1)
    nlev = len(_GLA_LEVELS)
    npair = GLA_HEADS // 2

    @pl.when(ci == 0)
    def _():
        z_ref[...] = jnp.zeros_like(z_ref)

    lf = lf_ref[...]
    hi = lf.astype(BF)
    lo = (lf - hi.astype(F32)).astype(BF)
    px_ref[...] = _dot(l_ref[...], hi) + _dot(l_ref[...], lo)
    q = q_ref[...]
    k = k_ref[...]
    lane = lax.broadcasted_iota(jnp.int32, (c, LANE), 1)
    low = lane < GLA_DK

    def put(idx, qv, kv):
        for p in range(npair):
            qp = qv[:, p * LANE:(p + 1) * LANE]
            qt_ref[idx, p, 0:c, :] = jnp.where(low, qp, 0.0).astype(BF)
            qt_ref[idx, p, c:2 * c, :] = jnp.where(low, 0.0, qp).astype(BF)
        kt_ref[idx] = kv.astype(BF)

    pblk = lambda i: px_ref[i * c:(i + 1) * c, :]
    put(0, q, k)
    put(1, q * jnp.exp(lf), k)
    for i in range(nlev - 1):
        put(2 + i, q * jnp.exp(pblk(i)), k * jnp.exp(pblk(nlev + i)))
    bcum = pblk(nlev - 1)
    put(nlev + 1, q * jnp.exp(bcum), k * jnp.exp(pblk(2 * nlev - 1)))
    decay = jnp.exp(bcum[c - 1:c, :])

    for p in range(npair):
        a = jnp.zeros((2 * c, c), F32)
        for lv in range(nlev + 1):
            mk = mask_ref[lv]
            sc = _dot_nt(qt_ref[lv, p], kt_ref[lv, :, p * LANE:(p + 1) * LANE])
            a = a + sc * jnp.concatenate([mk, mk], axis=0)
        kx = kt_ref[nlev + 1, :, p * LANE:(p + 1) * LANE]
        for hh in range(2):
            hd = 2 * p + hh
            vs = slice(hd * GLA_DV, (hd + 1) * GLA_DV)
            vh = v_ref[:, vs]
            z = z_ref[hd]
            o = (_dot(a[hh * c:(hh + 1) * c].astype(BF), vh)
                 + _dot_nt(qt_ref[nlev + 1, p, hh * c:(hh + 1) * c, :], z.astype(BF)))
            y = _rms(o, g_ref[...])
            hc_ref[:, vs] = (y * sg_ref[:, vs].astype(F32)).astype(hc_ref.dtype)
            z_ref[hd] = z * decay[:, p * LANE:(p + 1) * LANE] + _dot_tn(vh, kx)

    @pl.when(ci == nchunk - 1)
    def _():
        for hd in range(GLA_HEADS):
            zt = z_ref[hd].T
            s_ref[hd] = zt[(hd % 2) * GLA_DK:(hd % 2 + 1) * GLA_DK, :]


def _gla(gq, gk, gv, lf, sg, g_gla, batch, seq):
    c = GLA_CHUNK
    nchunk = seq // c
    lstack, masks = _gla_tables()
    nlev = len(_GLA_LEVELS)
    r3 = lambda a: a.reshape(batch, seq, a.shape[-1])
    blk = lambda n: pl.BlockSpec((None, c, n), lambda b, i: (b, i, 0))
    hc, s = pl.pallas_call(
        functools.partial(_gla_kernel, nchunk),
        grid=(batch, nchunk),
        in_specs=[blk(GLA_KW), blk(GLA_KW), blk(GLA_VW), blk(GLA_KW), blk(GLA_VW),
                  _const_spec(lstack.shape), _const_spec(masks.shape), _const_spec(g_gla.shape)],
        out_specs=[blk(GLA_VW), pl.BlockSpec((None, GLA_HEADS, GLA_DK, GLA_DV), lambda b, i: (b, 0, 0, 0))],
        out_shape=[jax.ShapeDtypeStruct((batch, seq, GLA_VW), BF),
                   jax.ShapeDtypeStruct((batch, GLA_HEADS, GLA_DK, GLA_DV), F32)],
        scratch_shapes=[pltpu.VMEM((2 * nlev * c, GLA_KW), F32),
                        pltpu.VMEM((nlev + 2, GLA_HEADS // 2, 2 * c, LANE), BF),
                        pltpu.VMEM((nlev + 2, c, GLA_KW), BF),
                        pltpu.VMEM((GLA_HEADS, GLA_DV, LANE), F32)],
        compiler_params=_params(("parallel", "arbitrary")),
        name="gla_prompt",
    )(r3(gq), r3(gk), r3(gv), r3(lf), r3(sg), lstack, masks, g_gla)
    return hc.reshape(batch * seq, GLA_VW), s


def _merge_kernel(x_ref, ha_ref, hb_ref, hc_ref, gmix_ref, wgate_ref, wpa_ref, wpb_ref, wpc_ref, wo_ref, o_ref):
    x = x_ref[...]
    h = _rms(x, gmix_ref[...]).astype(BF)
    mix = None
    for j, (hr, wr) in enumerate(((ha_ref, wpa_ref), (hb_ref, wpb_ref), (hc_ref, wpc_ref))):
        gate = jax.nn.sigmoid(_dot(h, wgate_ref[:, j * D_MODEL:(j + 1) * D_MODEL]))
        term = gate * _dot(hr[...].astype(BF), wr[...])
        mix = term if mix is None else mix + term
    o_ref[...] = x + _dot(mix.astype(BF), wo_ref[...])


def _merge(x, ha, hb, hc, w, tm):
    r = x.shape[0]
    row = lambda n: pl.BlockSpec((tm, n), lambda i: (i, 0))
    consts = (w['g_mix'], w['wgate'], w['wpa'], w['wpb'], w['wpc'], w['wo'])
    return pl.pallas_call(
        _merge_kernel,
        grid=(r // tm,),
        in_specs=[row(D_MODEL), row(ha.shape[1]), row(hb.shape[1]), row(hc.shape[1])]
                 + [_const_spec(c.shape) for c in consts],
        out_specs=row(D_MODEL),
        out_shape=jax.ShapeDtypeStruct((r, D_MODEL), F32),
        compiler_params=_params(("parallel",)),
        name="merge",
    )(x, ha, hb, hc, *consts)


def _ffn_kernel(final, x_ref, g_ref, wg_ref, wu_ref, wd_ref, gf_ref, o_ref):
    x = x_ref[...]
    hn = _rms(x, g_ref[...]).astype(BF)
    acc = jnp.zeros(x.shape, F32)
    for cidx in range(D_FF // FF_CHUNK):
        g = _dot(hn, wg_ref[cidx])
        u = _dot(hn, wu_ref[cidx])
        act = (g * jax.nn.sigmoid(g) * u).astype(BF)
        acc = acc + _dot(act, wd_ref[cidx])
    y = x + acc
    if final:
        y = _rms(y, gf_ref[...])
    o_ref[...] = y


def _ffn(x, w, g_final, final, tm):
    r = x.shape[0]
    row = pl.BlockSpec((tm, D_MODEL), lambda i: (i, 0))
    consts = (w['g_ffn'], w['wg'], w['wu'], w['wd'], g_final)
    return pl.pallas_call(
        functools.partial(_ffn_kernel, final),
        grid=(r // tm,),
        in_specs=[row] + [_const_spec(c.shape) for c in consts],
        out_specs=row,
        out_shape=jax.ShapeDtypeStruct((r, D_MODEL), F32),
        compiler_params=_params(("parallel",)),
        name="ffn",
    )(x, *consts)


PAGES_PER_STEP = 16


def _decode_kernel(npp, nsteps, pt_ref, ql_ref, qr_ref, cn_ref, kn_ref, *rest):
    ckv_refs = rest[:npp]
    kr_refs = rest[npp:2 * npp]
    o_ref = rest[2 * npp]
    cbuf, rbuf, m_ref, l_ref, acc_ref = rest[2 * npp + 1:]
    j = pl.program_id(1)
    ql = ql_ref[...]
    qr = qr_ref[...]

    @pl.when(j == 0)
    def _():
        cn = cn_ref[...]
        s_new = (jnp.sum(ql * cn, axis=-1, keepdims=True) + jnp.sum(qr * kn_ref[...], axis=-1, keepdims=True))
        m_ref[...] = s_new
        l_ref[...] = jnp.ones_like(l_ref)
        acc_ref[...] = jnp.broadcast_to(cn, acc_ref.shape)

    for i in range(npp):
        cbuf[i * PAGE_SIZE:(i + 1) * PAGE_SIZE, :] = ckv_refs[i][...].astype(BF)
        rbuf[i * PAGE_SIZE:(i + 1) * PAGE_SIZE, :] = kr_refs[i][...].astype(BF)
    s = _dot_nt(ql.astype(BF), cbuf[...]) + _dot_nt(qr.astype(BF), rbuf[...])
    m = m_ref[...]
    m_new = jnp.maximum(m, jnp.max(s, axis=-1, keepdims=True))
    alpha = jnp.exp(m - m_new)
    p = jnp.exp(s - m_new)
    l_ref[...] = alpha * l_ref[...] + jnp.sum(p, axis=-1, keepdims=True)
    acc_ref[...] = alpha * acc_ref[...] + _dot(p.astype(BF), cbuf[...])
    m_ref[...] = m_new

    @pl.when(j == nsteps - 1)
    def _():
        o_ref[...] = acc_ref[...] / l_ref[...]


def _decode(layer, page_table, ql, qr, ckv_new, kr_new, cache_ckv, cache_krope):
    b, npages = page_table.shape
    npp = PAGES_PER_STEP
    nsteps = npages // npp
    qmap = lambda bi, j, pt: (bi, 0, 0)

    def page_map(i):
        return lambda bi, j, pt: (layer, pt[bi, j * npp + i], 0, 0)

    in_specs = [pl.BlockSpec((None, MLA_HEADS, KV_RANK), qmap), pl.BlockSpec((None, MLA_HEADS, ROPE_DIM), qmap),
                pl.BlockSpec((None, 1, KV_RANK), qmap), pl.BlockSpec((None, 1, ROPE_DIM), qmap)]
    in_specs += [pl.BlockSpec((None, None, PAGE_SIZE, KV_RANK), page_map(i)) for i in range(npp)]
    in_specs += [pl.BlockSpec((None, None, PAGE_SIZE, ROPE_DIM), page_map(i)) for i in range(npp)]
    grid_spec = pltpu.PrefetchScalarGridSpec(
        num_scalar_prefetch=1, grid=(b, nsteps), in_specs=in_specs,
        out_specs=pl.BlockSpec((None, MLA_HEADS, KV_RANK), qmap),
        scratch_shapes=[pltpu.VMEM((npp * PAGE_SIZE, KV_RANK), BF), pltpu.VMEM((npp * PAGE_SIZE, ROPE_DIM), BF),
                        pltpu.VMEM((MLA_HEADS, 1), F32), pltpu.VMEM((MLA_HEADS, 1), F32),
                        pltpu.VMEM((MLA_HEADS, KV_RANK), F32)])
    return pl.pallas_call(
        functools.partial(_decode_kernel, npp, nsteps),
        grid_spec=grid_spec,
        out_shape=jax.ShapeDtypeStruct((b, MLA_HEADS, KV_RANK), F32),
        compiler_params=_params(("parallel", "arbitrary")),
        name="mla_decode",
    )(page_table, ql, qr, ckv_new, kr_new, *([cache_ckv] * npp), *([cache_krope] * npp))


def _sample_ha_kernel(ol_ref, w_ref, o_ref):
    acc = None
    for hd in range(MLA_HEADS):
        t = _dot(ol_ref[hd].astype(BF), w_ref[hd])
        acc = t if acc is None else acc + t
    o_ref[...] = acc


def _sample_ha(o_lat_t, wuv_wide):
    b = o_lat_t.shape[1]
    return pl.pallas_call(
        _sample_ha_kernel,
        out_shape=jax.ShapeDtypeStruct((b, MLA_HEADS * V_DIM), F32),
        compiler_params=pltpu.CompilerParams(vmem_limit_bytes=VMEM_LIMIT),
        name="sample_ha",
    )(o_lat_t, wuv_wide)


def _gla_step_kernel(q_ref, k_ref, v_ref, lf_ref, sg_ref, g_ref, s0_ref, hc_ref, s_ref):
    dk = GLA_DK
    eye = lax.broadcasted_iota(jnp.int32, (dk, dk), 0) == lax.broadcasted_iota(jnp.int32, (dk, dk), 1)
    col = lambda rowv: jnp.sum(jnp.where(eye, jnp.broadcast_to(rowv, (dk, dk)), 0.0), axis=-1, keepdims=True)
    for hd in range(GLA_HEADS):
        ks = slice(hd * dk, (hd + 1) * dk)
        vs = slice(hd * GLA_DV, (hd + 1) * GLA_DV)
        s_new = jnp.exp(col(lf_ref[:, ks])) * s0_ref[hd] + col(k_ref[:, ks]) * v_ref[:, vs]
        s_ref[hd] = s_new
        o = jnp.sum(col(q_ref[:, ks]) * s_new, axis=0, keepdims=True)
        hc_ref[:, vs] = _rms(o, g_ref[...]) * sg_ref[:, vs]


def _gla_step(gq, gk, gv, lf, sg, g_gla, s0):
    b = gq.shape[0]
    r3 = lambda a: a.reshape(b, 1, a.shape[-1])
    blk = lambda n: pl.BlockSpec((None, 1, n), lambda i: (i, 0, 0))
    sblk = pl.BlockSpec((None, GLA_HEADS, GLA_DK, GLA_DV), lambda i: (i, 0, 0, 0))
    hc, s = pl.pallas_call(
        _gla_step_kernel,
        grid=(b,),
        in_specs=[blk(GLA_KW), blk(GLA_KW), blk(GLA_VW), blk(GLA_KW), blk(GLA_VW), _const_spec(g_gla.shape), sblk],
        out_specs=[blk(GLA_VW), sblk],
        out_shape=[jax.ShapeDtypeStruct((b, 1, GLA_VW), F32), jax.ShapeDtypeStruct(s0.shape, F32)],
        compiler_params=_params(("parallel",)),
        name="gla_step",
    )(r3(gq), r3(gk), r3(gv), r3(lf), r3(sg), g_gla, s0)
    return hc.reshape(b, GLA_VW), s


def kernel(x_prompt, x_sample, cache_ckv, cache_krope, state_gla, page_table, g_mix, w_in, g_q, w_uq, g_kv, w_uk, w_uv, g_v, b_v, w_s, b_s, w_a2, b_a, g_gla, w_pa, w_pb, w_pc, w_o, g_ffn, w_gu, w_down, g_final):
    batch, seq, _ = x_prompt.shape
    dec_b, dec_t, _ = x_sample.shape
    assert dec_t == 1 and seq % 512 == 0
    depth = w_in.shape[0]
    past = page_table.shape[1] * PAGE_SIZE
    ct_p, st_p = _rope_tables(jnp.arange(seq, dtype=F32))
    ct_s, st_s = _rope_tables(jnp.full((dec_b,), float(past), F32))
    gfin = g_final.reshape(1, -1).astype(F32)

    xp = x_prompt.reshape(batch * seq, D_MODEL)
    xs = x_sample.reshape(dec_b, D_MODEL)
    ckv_p, kr_p, gla_p, ckv_s, kr_s, gla_s, gv_s = [], [], [], [], [], [], []
    for l in range(depth):
        w = _prep_layer(l, g_mix, w_in, g_q, w_uq, g_kv, w_uk, w_uv, g_v, b_v, w_s, b_s, w_a2, b_a, g_gla,
                        w_pa, w_pb, w_pc, w_o, g_ffn, w_gu, w_down)
        last = l == depth - 1
        q, k, v, ckv, kr, hb, gq, gk, gv, lf, sg = _front(xp, ct_p, st_p, w, True, seq)
        ha = _attention(q, k, v, batch, seq)
        hc, s_c = _gla(gq, gk, gv, lf, sg, w['g_gla'], batch, seq)
        xp = _ffn(_merge(xp, ha, hb, hc, w, 512), w, gfin, last, 512)
        ckv_p.append(ckv.reshape(batch, seq, KV_RANK))
        kr_p.append(kr.reshape(batch, seq, ROPE_DIM))
        gla_p.append(s_c)
        q, ql, ckv, kr, hb, vn, gq, gk, gv, lf, sg = _front(xs, ct_s, st_s, w, False, seq)
        qr = q.reshape(dec_b, MLA_HEADS, HEAD_PAD)[:, :, NOPE_DIM:NOPE_DIM + ROPE_DIM]
        o_lat = _decode(l, page_table, ql.reshape(dec_b, MLA_HEADS, KV_RANK), qr,
                        ckv.reshape(dec_b, 1, KV_RANK), kr.reshape(dec_b, 1, ROPE_DIM), cache_ckv, cache_krope)
        ha = _sample_ha(o_lat.transpose(1, 0, 2), w['wuv_wide'])
        hc, s_c = _gla_step(gq, gk, gv, lf, sg, w['g_gla'], state_gla[l])
        xs = _ffn(_merge(xs, ha, hb, hc, w, dec_b), w, gfin, last, dec_b)
        ckv_s.append(ckv.reshape(dec_b, 1, KV_RANK))
        kr_s.append(kr.reshape(dec_b, 1, ROPE_DIM))
        gla_s.append(s_c)
        gv_s.append(vn.reshape(dec_b, 1, GMLP_WIDTH))
    return (xp.reshape(batch, seq, D_MODEL), xs.reshape(dec_b, 1, D_MODEL), jnp.stack(ckv_p), jnp.stack(kr_p),
            jnp.stack(gla_p), jnp.stack(ckv_s), jnp.stack(kr_s), jnp.stack(gla_s), jnp.stack(gv_s))
```

```python
import functools
import math

import numpy as np
import jax
import jax.numpy as jnp
from jax import lax
from jax.experimental import pallas as pl
from jax.experimental.pallas import tpu as pltpu

D_MODEL = 1024
PAGE_SIZE = 128
MLA_HEADS = 8
Q_RANK = 384
KV_RANK = 256
NOPE_DIM = 64
ROPE_DIM = 32
V_DIM = 64
ROPE_THETA = 10000.0
GMLP_GROUPS = 8
GMLP_CHUNK = 128
GMLP_WIDTH = 512
GLA_HEADS = 4
GLA_DK = 64
GLA_DV = 128
GLA_KW = GLA_HEADS * GLA_DK
GLA_VW = GLA_HEADS * GLA_DV
GLA_GATE_RANK = 16
GLA_TAU = 16.0
N_BRANCH = 3
D_FF = 2816
IN_SPLITS = (Q_RANK, KV_RANK, ROPE_DIM, GMLP_WIDTH, GMLP_WIDTH, GLA_KW, GLA_KW, GLA_VW,
             GLA_GATE_RANK, GLA_VW, N_BRANCH * D_MODEL)

LANE = 128
HEAD_PAD = 128
GLA_CHUNK = 128
ROW_TILE = 512
ATTN_TILE = 256
ATTN_Q_TILE = 512
FF_CHUNK = 256
VMEM_LIMIT = 56 * 1024 * 1024

BF = jnp.bfloat16
F32 = jnp.float32


def _dot(a, b):
    return jnp.dot(a, b, preferred_element_type=F32)


def _dot_nt(a, b):
    return lax.dot_general(a, b, (((1,), (1,)), ((), ())), preferred_element_type=F32)


def _dot_tn(a, b):
    return lax.dot_general(a, b, (((0,), (0,)), ((), ())), preferred_element_type=F32)


def _rms(x, g, eps=1e-6):
    return x * lax.rsqrt(jnp.mean(x * x, axis=-1, keepdims=True) + eps) * g


def _const_spec(shape):
    nd = len(shape)
    return pl.BlockSpec(shape, lambda *_: (0,) * nd, pipeline_mode=pl.Buffered(1))


def _params(sem):
    return pltpu.CompilerParams(dimension_semantics=sem, vmem_limit_bytes=VMEM_LIMIT)


def _prep_layer(l, g_mix, w_in, g_q, w_uq, g_kv, w_uk, w_uv, g_v, b_v, w_s, b_s, w_a2, b_a, g_gla,
                w_pa, w_pb, w_pc, w_o, g_ffn, w_gu, w_down):
    o = np.cumsum((0,) + IN_SPLITS)
    wi = w_in[l]
    cols = lambda i: wi[:, o[i]:o[i + 1]]
    half = ROPE_DIM // 2
    kr = cols(2)
    kr_swap = jnp.concatenate([-kr[:, half:], kr[:, :half]], axis=1)
    z16 = jnp.zeros((D_MODEL, 16), F32)
    z32 = jnp.zeros((D_MODEL, 32), F32)
    small = jnp.concatenate([kr_swap, cols(8), z16, kr, z32], axis=1)
    wf = jnp.concatenate([cols(0), cols(1), small, cols(3), cols(4), cols(5), cols(6), cols(7), cols(9)],
                         axis=1).astype(BF)
    wgate = cols(10).astype(BF)

    uq = w_uq[l]
    qn, qr = uq[..., :NOPE_DIM], uq[..., NOPE_DIM:]
    qr_swap = jnp.concatenate([-qr[..., half:], qr[..., :half]], axis=-1)
    zq = jnp.zeros((Q_RANK, MLA_HEADS, HEAD_PAD - NOPE_DIM - ROPE_DIM), F32)
    wq_cat = jnp.concatenate([qn, qr, zq], axis=-1).reshape(Q_RANK, MLA_HEADS * HEAD_PAD).astype(BF)
    wq_swap = jnp.concatenate([jnp.zeros_like(qn), qr_swap, zq], axis=-1).reshape(Q_RANK, MLA_HEADS * HEAD_PAD).astype(BF)

    uk = w_uk[l]
    wk_pad = jnp.concatenate([uk, jnp.zeros_like(uk)], axis=-1).reshape(KV_RANK, MLA_HEADS * HEAD_PAD).astype(BF)
    uv = w_uv[l]
    zv = jnp.zeros_like(uv)
    even = (jnp.arange(MLA_HEADS) % 2 == 0)[None, :, None]
    wv_pad = jnp.concatenate([jnp.where(even, uv, zv), jnp.where(even, zv, uv)], axis=-1)
    wvt_pad = wv_pad.reshape(KV_RANK, MLA_HEADS * HEAD_PAD).T.astype(BF)
    wukt = jnp.concatenate([uk.transpose(1, 2, 0), jnp.zeros((MLA_HEADS, HEAD_PAD - NOPE_DIM, KV_RANK), F32)],
                           axis=1).astype(BF)
    eye_h = jnp.eye(MLA_HEADS, dtype=F32)
    wuv_wide = jnp.einsum('rhd,hg->hrgd', uv, eye_h).reshape(MLA_HEADS, KV_RANK, MLA_HEADS * V_DIM).astype(BF)

    wa2_pad = jnp.zeros((LANE, GLA_KW), F32).at[32:32 + GLA_GATE_RANK].set(w_a2[l]).astype(BF)
    tri = jnp.tril(jnp.ones((GMLP_CHUNK, GMLP_CHUNK), bool))
    ws = jnp.where(tri[None], w_s[l], 0.0).astype(BF)
    gd = GMLP_WIDTH // GMLP_GROUPS
    bias_tab = jnp.repeat(b_s[l].T, gd, axis=1)
    ws00 = jnp.repeat(w_s[l][:, 0, 0], gd)[None, :]
    bs0 = jnp.repeat(b_s[l][:, 0], gd)[None, :]

    nff = D_FF // FF_CHUNK
    wg = w_gu[l][:, :D_FF].reshape(D_MODEL, nff, FF_CHUNK).transpose(1, 0, 2).astype(BF)
    wu = w_gu[l][:, D_FF:].reshape(D_MODEL, nff, FF_CHUNK).transpose(1, 0, 2).astype(BF)
    wd = w_down[l].reshape(nff, FF_CHUNK, D_MODEL).astype(BF)
    row = lambda a: a.reshape(1, -1).astype(F32)
    return dict(
        wf=wf, wgate=wgate, wq_cat=wq_cat, wq_swap=wq_swap, wk_pad=wk_pad, wvt_pad=wvt_pad, wukt=wukt,
        wuv_wide=wuv_wide, wa2_pad=wa2_pad, ws=ws, bias_tab=bias_tab, ws00=ws00, bs0=bs0,
        g_mix=row(g_mix[l]), g_q=row(g_q[l]), g_kv=row(g_kv[l]), g_v=row(g_v[l]), b_v=row(b_v[l]),
        b_a=row(b_a[l]), g_gla=row(g_gla[l]), g_ffn=row(g_ffn[l]),
        wpa=w_pa[l].astype(BF), wpb=w_pb[l].astype(BF), wpc=w_pc[l].astype(BF), wo=w_o[l].astype(BF),
        wg=wg, wu=wu, wd=wd)


def _rope_tables(pos):
    half = ROPE_DIM // 2
    inv = jnp.power(ROPE_THETA, -jnp.arange(half, dtype=F32) * 2.0 / ROPE_DIM)
    ang = pos[:, None] * inv[None, :]
    cos, sin = jnp.cos(ang), jnp.sin(ang)
    t = pos.shape[0]
    ctab = jnp.concatenate([jnp.ones((t, NOPE_DIM), F32), cos, cos, jnp.zeros((t, 32), F32)], axis=1)
    stab = jnp.concatenate([jnp.zeros((t, NOPE_DIM), F32), sin, sin, jnp.zeros((t, 32), F32)], axis=1)
    return ctab, stab


_WF_OFF = np.cumsum((0, Q_RANK, KV_RANK, LANE, GMLP_WIDTH, GMLP_WIDTH, GLA_KW, GLA_KW, GLA_VW, GLA_VW))


def _front_kernel(prompt, tm, x_ref, ct_ref, st_ref, wf_ref, wq_cat_ref, wq_swap_ref, wkv_a_ref, wkv_b_ref,
                  wa2_ref, gm_a_ref, gm_b_ref, gmix_ref, gq_ref, gkv_ref, gv_ref, bv_ref, ba_ref, *outs):
    if prompt:
        (q_ref, k_ref, v_ref, ckv_ref, kr_ref, hb_ref, oq_ref, ok_ref, ov_ref, lf_ref, sg_ref) = outs
    else:
        (q_ref, ql_ref, ckv_ref, kr_ref, hb_ref, vn_ref, oq_ref, ok_ref, ov_ref, lf_ref, sg_ref) = outs
    wcol = lambda i: wf_ref[:, _WF_OFF[i]:_WF_OFF[i + 1]]
    x = x_ref[...]
    h = _rms(x, gmix_ref[...]).astype(BF)
    ct = ct_ref[...]
    st = st_ref[...]
    scale = (NOPE_DIM + ROPE_DIM) ** -0.5 * math.log2(math.e)

    cq = _rms(_dot(h, wcol(0)), gq_ref[...]).astype(BF)
    qa = _dot(cq, wq_cat_ref[...])
    qb = _dot(cq, wq_swap_ref[...])
    for hd in range(MLA_HEADS):
        sl = slice(hd * HEAD_PAD, (hd + 1) * HEAD_PAD)
        qh = (qa[:, sl] * ct + qb[:, sl] * st) * scale
        q_ref[:, sl] = qh.astype(q_ref.dtype)
        if not prompt:
            ql_ref[:, hd * KV_RANK:(hd + 1) * KV_RANK] = _dot(qh.astype(BF), wkv_a_ref[hd])

    ckv = _rms(_dot(h, wcol(1)), gkv_ref[...])
    ckv_ref[...] = ckv
    small = _dot(h, wcol(2))
    lane = lax.broadcasted_iota(jnp.int32, small.shape, 1)
    kr_rot = jnp.where((lane >= NOPE_DIM) & (lane < NOPE_DIM + ROPE_DIM),
                       small * ct + pltpu.roll(small, NOPE_DIM, 1) * st, 0.0)
    kr_ref[...] = pltpu.roll(kr_rot, NOPE_DIM, 1)[:, :ROPE_DIM]
    if prompt:
        ckvb = ckv.astype(BF)
        kn = _dot(ckvb, wkv_a_ref[...])
        for hd in range(MLA_HEADS):
            sl = slice(hd * HEAD_PAD, (hd + 1) * HEAD_PAD)
            k_ref[:, sl] = (kn[:, sl] + kr_rot).astype(BF)
        vt = _dot_nt(wkv_b_ref[...], ckvb)
        rowi = lax.broadcasted_iota(jnp.int32, vt.shape, 0)
        is_pad = ((rowi // V_DIM) % 2) != ((rowi // HEAD_PAD) % 2)
        vt = jnp.where(is_pad, 1.0, vt).astype(BF)
        for t in range(tm // ATTN_TILE):
            v_ref[t] = vt[:, t * ATTN_TILE:(t + 1) * ATTN_TILE]

    u = _dot(h, wcol(3))
    v = _dot(h, wcol(4))
    mu = jnp.mean(v, axis=-1, keepdims=True)
    vc = v - mu
    var = jnp.mean(vc * vc, axis=-1, keepdims=True)
    vn = vc * lax.rsqrt(var + 1e-5) * gv_ref[...] + bv_ref[...]
    if prompt:
        vnb = vn.astype(BF)
        lane2 = lax.broadcasted_iota(jnp.int32, (GMLP_CHUNK, LANE), 1)
        gd = GMLP_WIDTH // GMLP_GROUPS
        for c in range(tm // GMLP_CHUNK):
            rs = slice(c * GMLP_CHUNK, (c + 1) * GMLP_CHUNK)
            for gp in range(GMLP_GROUPS // 2):
                cs = slice(gp * LANE, (gp + 1) * LANE)
                vp = vnb[rs, cs]
                mixed = jnp.where(lane2 < gd, _dot(gm_a_ref[2 * gp], vp), _dot(gm_a_ref[2 * gp + 1], vp))
                hb_ref[rs, cs] = (u[rs, cs] * (mixed + gm_b_ref[:, cs])).astype(hb_ref.dtype)
    else:
        vn_ref[...] = vn
        hb_ref[...] = (u * (gm_a_ref[...] * vn + gm_b_ref[...])).astype(hb_ref.dtype)

    oq_ref[...] = _dot(h, wcol(5)) * (GLA_DK ** -0.5)
    ok_ref[...] = _dot(h, wcol(6))
    ov_ref[...] = _dot(h, wcol(7)).astype(ov_ref.dtype)
    a = _dot(small.astype(BF), wa2_ref[...]) + ba_ref[...]
    lf_ref[...] = (jnp.minimum(a, 0.0) - jnp.log(1.0 + jnp.exp(-jnp.abs(a)))) * (1.0 / GLA_TAU)
    gg = _dot(h, wcol(8))
    sg_ref[...] = (gg * jax.nn.sigmoid(gg)).astype(sg_ref.dtype)


def _front(x, ctab, stab, w, prompt, seq):
    r = x.shape[0]
    tm = ROW_TILE if prompt else r
    nt = r // tm
    row = lambda n: pl.BlockSpec((tm, n), lambda i: (i, 0))
    rows = lambda n, dt: (row(n), jax.ShapeDtypeStruct((r, n), dt))
    width = MLA_HEADS * HEAD_PAD
    if prompt:
        tab = pl.BlockSpec((tm, LANE), lambda i: (i % (seq // tm), 0))
        wkv_a, wkv_b = w['wk_pad'], w['wvt_pad']
        gm_a, gm_b = w['ws'], w['bias_tab']
        vt_out = (pl.BlockSpec((tm // ATTN_TILE, width, ATTN_TILE), lambda i: (i, 0, 0)),
                  jax.ShapeDtypeStruct((r // ATTN_TILE, width, ATTN_TILE), BF))
        outs = (rows(width, BF), rows(width, BF), vt_out, rows(KV_RANK, F32), rows(ROPE_DIM, F32),
                rows(GMLP_WIDTH, BF), rows(GLA_KW, F32), rows(GLA_KW, F32), rows(GLA_VW, BF), rows(GLA_KW, F32),
                rows(GLA_VW, BF))
    else:
        tab = pl.BlockSpec((tm, LANE), lambda i: (0, 0))
        wkv_a, wkv_b = w['wukt'], w['wvt_pad']
        gm_a, gm_b = w['ws00'], w['bs0']
        outs = (rows(width, F32), rows(MLA_HEADS * KV_RANK, F32), rows(KV_RANK, F32), rows(ROPE_DIM, F32),
                rows(GMLP_WIDTH, BF), rows(GMLP_WIDTH, F32), rows(GLA_KW, F32), rows(GLA_KW, F32), rows(GLA_VW, F32),
                rows(GLA_KW, F32), rows(GLA_VW, F32))
    consts = (w['wf'], w['wq_cat'], w['wq_swap'], wkv_a, wkv_b, w['wa2_pad'], gm_a, gm_b,
              w['g_mix'], w['g_q'], w['g_kv'], w['g_v'], w['b_v'], w['b_a'])
    return pl.pallas_call(
        functools.partial(_front_kernel, prompt, tm),
        grid=(nt,),
        in_specs=[row(D_MODEL), tab, tab] + [_const_spec(c.shape) for c in consts],
        out_specs=[o[0] for o in outs],
        out_shape=[o[1] for o in outs],
        compiler_params=_params(("parallel",)),
        name="front_prompt" if prompt else "front_sample",
    )(x, ctab, stab, *consts)


def _attn_kernel(q_ref, k_ref, vt_ref, o_ref, m_ref, acc_ref, st_ref, p_ref):
    tk, tq = ATTN_TILE, ATTN_Q_TILE
    ratio = tq // tk
    qi = pl.program_id(1)
    m_ref[...] = jnp.full(m_ref.shape, -jnp.inf, F32)
    acc_ref[...] = jnp.zeros(acc_ref.shape, F32)
    krow = lax.broadcasted_iota(jnp.int32, (tk, tq), 0)
    qcol = lax.broadcasted_iota(jnp.int32, (tk, tq), 1)

    def tile(j, diag):
        off = pl.multiple_of(j * tk, tk)
        heads = [slice(hd * HEAD_PAD, (hd + 1) * HEAD_PAD) for hd in range(MLA_HEADS)]
        mx = []
        for hd, cs in enumerate(heads):
            st = _dot_nt(k_ref[pl.ds(off, tk), cs], q_ref[:, cs])
            if diag is not None:
                st = jnp.where(krow + diag * tk <= qcol, st, -jnp.inf)
            st_ref[hd] = st
            mx.append(jnp.max(st, axis=0, keepdims=True))
        alpha = []
        for hd, cs in enumerate(heads):
            m = m_ref[hd]
            m_new = jnp.maximum(m, mx[hd])
            alpha.append(jnp.exp2(m - m_new))
            m_ref[hd] = m_new
            p_ref[hd] = jnp.exp2(st_ref[hd] - m_new).astype(BF)
        for hd, cs in enumerate(heads):
            acc_ref[hd] = alpha[hd] * acc_ref[hd] + _dot(vt_ref[j, cs, :], p_ref[hd])

    def body(j, carry):
        tile(j, None)
        return carry

    lax.fori_loop(0, qi * ratio, body, 0)
    for d in range(ratio):
        tile(qi * ratio + d, d)
    rowi = lax.broadcasted_iota(jnp.int32, (HEAD_PAD, tq), 0)
    for pr in range(MLA_HEADS // 2):
        ae = acc_ref[2 * pr]
        ao = acc_ref[2 * pr + 1]
        out_t = jnp.where(rowi < V_DIM, ae / ae[V_DIM:V_DIM + 1, :], ao / ao[0:1, :])
        o_ref[:, pr * HEAD_PAD:(pr + 1) * HEAD_PAD] = out_t.T.astype(o_ref.dtype)


def _attention(q, k, vt, batch, seq):
    tk, tq = ATTN_TILE, ATTN_Q_TILE
    width = MLA_HEADS * HEAD_PAD
    q3, k3 = (a.reshape(batch, seq, width) for a in (q, k))
    vt4 = vt.reshape(batch, seq // tk, width, tk)
    out = pl.pallas_call(
        _attn_kernel,
        grid=(batch, seq // tq),
        in_specs=[pl.BlockSpec((None, tq, width), lambda b, i: (b, i, 0)),
                  pl.BlockSpec((None, seq, width), lambda b, i: (b, 0, 0)),
                  pl.BlockSpec((None, seq // tk, width, tk), lambda b, i: (b, 0, 0, 0))],
        out_specs=pl.BlockSpec((None, tq, MLA_HEADS * V_DIM), lambda b, i: (b, i, 0)),
        out_shape=jax.ShapeDtypeStruct((batch, seq, MLA_HEADS * V_DIM), BF),
        scratch_shapes=[pltpu.VMEM((MLA_HEADS, 1, tq), F32), pltpu.VMEM((MLA_HEADS, HEAD_PAD, tq), F32),
                        pltpu.VMEM((MLA_HEADS, tk, tq), F32), pltpu.VMEM((MLA_HEADS, tk, tq), BF)],
        compiler_params=_params(("parallel", "arbitrary")),
        name="mla_prompt_attn",
    )(q3, k3, vt4)
    return out.reshape(batch * seq, MLA_HEADS * V_DIM)


_GLA_LEVELS = (1, 2, 4, 8, 16, 32, 64)


def _gla_tables():
    c = GLA_CHUNK
    t = np.arange(c)[:, None]
    r = np.arange(c)[None, :]
    blocks = []
    for m in _GLA_LEVELS[1:] + (c,):
        blocks.append(((t // m == r // m) & (r <= t)))
    for m in _GLA_LEVELS[1:] + (c,):
        blocks.append(((t // m == r // m) & (r > t)))
    lstack = np.concatenate(blocks, axis=0).astype(np.float32)
    masks = [np.eye(c, dtype=np.float32)]
    for m in _GLA_LEVELS:
        masks.append((((t // m) % 2 == 1) & (r // m == t // m - 1)).astype(np.float32))
    return jnp.asarray(lstack, BF), jnp.asarray(np.stack(masks), F32)


def _gla_kernel(nchunk, q_ref, k_ref, v_ref, lf_ref, sg_ref, l_ref, mask_ref, g_ref, hc_ref, s_ref,
                px_ref, qt_ref, kt_ref, z_ref):
    c = GLA_CHUNK
    ci = pl.program_id(1)
    nlev = len(_GLA_LEVELS)
    npair = GLA_HEADS // 2

    @pl.when(ci == 0)
    def _():
        z_ref[...] = jnp.zeros_like(z_ref)

    lf = lf_ref[...]
    hi = lf.astype(BF)
    lo = (lf - hi.astype(F32)).astype(BF)
    px_ref[...] = _dot(l_ref[...], hi) + _dot(l_ref[...], lo)
    q = q_ref[...]
    k = k_ref[...]
    lane = lax.broadcasted_iota(jnp.int32, (c, LANE), 1)
    low = lane < GLA_DK

    def put(idx, qv, kv):
        for p in range(npair):
            qp = qv[:, p * LANE:(p + 1) * LANE]
            qt_ref[idx, p, 0:c, :] = jnp.where(low, qp, 0.0).astype(BF)
            qt_ref[idx, p, c:2 * c, :] = jnp.where(low, 0.0, qp).astype(BF)
        kt_ref[idx] = kv.astype(BF)

    pblk = lambda i: px_ref[i * c:(i + 1) * c, :]
    put(0, q, k)
    put(1, q * jnp.exp(lf), k)
    for i in range(nlev - 1):
        put(2 + i, q * jnp.exp(pblk(i)), k * jnp.exp(pblk(nlev + i)))
    bcum = pblk(nlev - 1)
    put(nlev + 1, q * jnp.exp(bcum), k * jnp.exp(pblk(2 * nlev - 1)))
    decay = jnp.exp(bcum[c - 1:c, :])

    for p in range(npair):
        a = jnp.zeros((2 * c, c), F32)
        for lv in range(nlev + 1):
            mk = mask_ref[lv]
            sc = _dot_nt(qt_ref[lv, p], kt_ref[lv, :, p * LANE:(p + 1) * LANE])
            a = a + sc * jnp.concatenate([mk, mk], axis=0)
        kx = kt_ref[nlev + 1, :, p * LANE:(p + 1) * LANE]
        for hh in range(2):
            hd = 2 * p + hh
            vs = slice(hd * GLA_DV, (hd + 1) * GLA_DV)
            vh = v_ref[:, vs]
            z = z_ref[hd]
            o = (_dot(a[hh * c:(hh + 1) * c].astype(BF), vh)
                 + _dot_nt(qt_ref[nlev + 1, p, hh * c:(hh + 1) * c, :], z.astype(BF)))
            y = _rms(o, g_ref[...])
            hc_ref[:, vs] = (y * sg_ref[:, vs].astype(F32)).astype(hc_ref.dtype)
            z_ref[hd] = z * decay[:, p * LANE:(p + 1) * LANE] + _dot_tn(vh, kx)

    @pl.when(ci == nchunk - 1)
    def _():
        for hd in range(GLA_HEADS):
            zt = z_ref[hd].T
            s_ref[hd] = zt[(hd % 2) * GLA_DK:(hd % 2 + 1) * GLA_DK, :]


def _gla(gq, gk, gv, lf, sg, g_gla, batch, seq):
    c = GLA_CHUNK
    nchunk = seq // c
    lstack, masks = _gla_tables()
    nlev = len(_GLA_LEVELS)
    r3 = lambda a: a.reshape(batch, seq, a.shape[-1])
    blk = lambda n: pl.BlockSpec((None, c, n), lambda b, i: (b, i, 0))
    hc, s = pl.pallas_call(
        functools.partial(_gla_kernel, nchunk),
        grid=(batch, nchunk),
        in_specs=[blk(GLA_KW), blk(GLA_KW), blk(GLA_VW), blk(GLA_KW), blk(GLA_VW),
                  _const_spec(lstack.shape), _const_spec(masks.shape), _const_spec(g_gla.shape)],
        out_specs=[blk(GLA_VW), pl.BlockSpec((None, GLA_HEADS, GLA_DK, GLA_DV), lambda b, i: (b, 0, 0, 0))],
        out_shape=[jax.ShapeDtypeStruct((batch, seq, GLA_VW), BF),
                   jax.ShapeDtypeStruct((batch, GLA_HEADS, GLA_DK, GLA_DV), F32)],
        scratch_shapes=[pltpu.VMEM((2 * nlev * c, GLA_KW), F32),
                        pltpu.VMEM((nlev + 2, GLA_HEADS // 2, 2 * c, LANE), BF),
                        pltpu.VMEM((nlev + 2, c, GLA_KW), BF),
                        pltpu.VMEM((GLA_HEADS, GLA_DV, LANE), F32)],
        compiler_params=_params(("parallel", "arbitrary")),
        name="gla_prompt",
    )(r3(gq), r3(gk), r3(gv), r3(lf), r3(sg), lstack, masks, g_gla)
    return hc.reshape(batch * seq, GLA_VW), s


def _merge_kernel(x_ref, ha_ref, hb_ref, hc_ref, gmix_ref, wgate_ref, wpa_ref, wpb_ref, wpc_ref, wo_ref, o_ref):
    x = x_ref[...]
    h = _rms(x, gmix_ref[...]).astype(BF)
    mix = None
    for j, (hr, wr) in enumerate(((ha_ref, wpa_ref), (hb_ref, wpb_ref), (hc_ref, wpc_ref))):
        gate = jax.nn.sigmoid(_dot(h, wgate_ref[:, j * D_MODEL:(j + 1) * D_MODEL]))
        term = gate * _dot(hr[...].astype(BF), wr[...])
        mix = term if mix is None else mix + term
    o_ref[...] = x + _dot(mix.astype(BF), wo_ref[...])


def _merge(x, ha, hb, hc, w, tm):
    r = x.shape[0]
    row = lambda n: pl.BlockSpec((tm, n), lambda i: (i, 0))
    consts = (w['g_mix'], w['wgate'], w['wpa'], w['wpb'], w['wpc'], w['wo'])
    return pl.pallas_call(
        _merge_kernel,
        grid=(r // tm,),
        in_specs=[row(D_MODEL), row(ha.shape[1]), row(hb.shape[1]), row(hc.shape[1])]
                 + [_const_spec(c.shape) for c in consts],
        out_specs=row(D_MODEL),
        out_shape=jax.ShapeDtypeStruct((r, D_MODEL), F32),
        compiler_params=_params(("parallel",)),
        name="merge",
    )(x, ha, hb, hc, *consts)


def _ffn_kernel(final, x_ref, g_ref, wg_ref, wu_ref, wd_ref, gf_ref, o_ref):
    x = x_ref[...]
    hn = _rms(x, g_ref[...]).astype(BF)
    acc = jnp.zeros(x.shape, F32)
    for cidx in range(D_FF // FF_CHUNK):
        g = _dot(hn, wg_ref[cidx])
        u = _dot(hn, wu_ref[cidx])
        act = (g * jax.nn.sigmoid(g) * u).astype(BF)
        acc = acc + _dot(act, wd_ref[cidx])
    y = x + acc
    if final:
        y = _rms(y, gf_ref[...])
    o_ref[...] = y


def _ffn(x, w, g_final, final, tm):
    r = x.shape[0]
    row = pl.BlockSpec((tm, D_MODEL), lambda i: (i, 0))
    consts = (w['g_ffn'], w['wg'], w['wu'], w['wd'], g_final)
    return pl.pallas_call(
        functools.partial(_ffn_kernel, final),
        grid=(r // tm,),
        in_specs=[row] + [_const_spec(c.shape) for c in consts],
        out_specs=row,
        out_shape=jax.ShapeDtypeStruct((r, D_MODEL), F32),
        compiler_params=_params(("parallel",)),
        name="ffn",
    )(x, *consts)


DECODE_PAGES = 32
PAGE_GROUP = 8


def _decode_kernel(layer, nch, pt_ref, ql_ref, qr_ref, cn_ref, kn_ref, ckv_hbm, krt_hbm, o_ref,
                   cbuf, rbuf, cb, rb, sem, m_ref, l_ref, acc_ref):
    npg = DECODE_PAGES
    total = ql_ref.shape[0] * nch

    def page_copies(slot, i, pid):
        rows = pl.ds(i * PAGE_SIZE, PAGE_SIZE)
        return (pltpu.make_async_copy(ckv_hbm.at[layer, pid], cbuf.at[slot, rows, :], sem.at[0, slot]),
                pltpu.make_async_copy(krt_hbm.at[layer, pid], rbuf.at[slot, :, rows], sem.at[1, slot]))

    def start(step, slot):
        b = step // nch
        c = step % nch
        for i in range(npg):
            for cp in page_copies(slot, i, pt_ref[b, c * npg + i]):
                cp.start()

    def wait(slot):
        for i in range(npg):
            for cp in page_copies(slot, i, 0):
                cp.wait()

    start(0, 0)

    def body(step, carry):
        slot = step % 2
        b = step // nch
        c = step % nch

        @pl.when(step + 1 < total)
        def _():
            start(step + 1, 1 - slot)

        ql = ql_ref[b]
        qr = qr_ref[b]

        @pl.when(c == 0)
        def _():
            cn = cn_ref[b]
            m_ref[...] = (jnp.sum(ql * cn, axis=-1, keepdims=True) + jnp.sum(qr * kn_ref[b], axis=-1, keepdims=True))
            l_ref[...] = jnp.ones_like(l_ref)
            acc_ref[...] = jnp.broadcast_to(cn, acc_ref.shape)

        wait(slot)
        qlb = ql.astype(BF)
        qrb = qr.astype(BF)
        span = PAGE_GROUP * PAGE_SIZE
        ss = []
        for g in range(npg // PAGE_GROUP):
            rows = slice(g * span, (g + 1) * span)
            cb[rows, :] = cbuf[slot, rows, :].astype(BF)
            rb[:, rows] = rbuf[slot, :, rows].astype(BF)
            ss.append(_dot_nt(qlb, cb[rows, :]) + _dot(qrb, rb[:, rows]))
        mc = ss[0].max(axis=-1, keepdims=True)
        for s in ss[1:]:
            mc = jnp.maximum(mc, s.max(axis=-1, keepdims=True))
        m = m_ref[...]
        m_new = jnp.maximum(m, mc)
        alpha = jnp.exp2(m - m_new)
        l = alpha * l_ref[...]
        acc = alpha * acc_ref[...]
        for g, s in enumerate(ss):
            p = jnp.exp2(s - m_new)
            l = l + jnp.sum(p, axis=-1, keepdims=True)
            acc = acc + _dot(p.astype(BF), cb[g * span:(g + 1) * span, :])
        m_ref[...] = m_new
        l_ref[...] = l
        acc_ref[...] = acc

        @pl.when(c == nch - 1)
        def _():
            o_ref[b] = acc / l

        return carry

    lax.fori_loop(0, total, body, 0)


def _decode(layer, page_table, ql, qr, ckv_new, kr_new, cache_ckv, cache_krope_t):
    b, npages = page_table.shape
    npg = DECODE_PAGES
    assert npages % npg == 0 and npg % PAGE_GROUP == 0
    keys = npg * PAGE_SIZE
    vmem = pl.BlockSpec(memory_space=pltpu.VMEM)
    hbm = pl.BlockSpec(memory_space=pl.ANY)
    return pl.pallas_call(
        functools.partial(_decode_kernel, layer, npages // npg),
        in_specs=[pl.BlockSpec(memory_space=pltpu.SMEM), vmem, vmem, vmem, vmem, hbm, hbm],
        out_specs=vmem,
        out_shape=jax.ShapeDtypeStruct((b, MLA_HEADS, KV_RANK), F32),
        scratch_shapes=[pltpu.VMEM((2, keys, KV_RANK), F32), pltpu.VMEM((2, ROPE_DIM, keys), F32),
                        pltpu.VMEM((keys, KV_RANK), BF), pltpu.VMEM((ROPE_DIM, keys), BF),
                        pltpu.SemaphoreType.DMA((2, 2)),
                        pltpu.VMEM((MLA_HEADS, 1), F32), pltpu.VMEM((MLA_HEADS, 1), F32),
                        pltpu.VMEM((MLA_HEADS, KV_RANK), F32)],
        compiler_params=pltpu.CompilerParams(vmem_limit_bytes=VMEM_LIMIT),
        name="mla_decode",
    )(page_table, ql, qr, ckv_new, kr_new, cache_ckv, cache_krope_t)


def _sample_ha_kernel(ol_ref, w_ref, o_ref):
    acc = None
    for hd in range(MLA_HEADS):
        t = _dot(ol_ref[hd].astype(BF), w_ref[hd])
        acc = t if acc is None else acc + t
    o_ref[...] = acc


def _sample_ha(o_lat_t, wuv_wide):
    b = o_lat_t.shape[1]
    return pl.pallas_call(
        _sample_ha_kernel,
        out_shape=jax.ShapeDtypeStruct((b, MLA_HEADS * V_DIM), F32),
        compiler_params=pltpu.CompilerParams(vmem_limit_bytes=VMEM_LIMIT),
        name="sample_ha",
    )(o_lat_t, wuv_wide)


def _gla_step_kernel(q_ref, k_ref, v_ref, lf_ref, sg_ref, g_ref, s0_ref, hc_ref, s_ref):
    dk = GLA_DK
    eye = lax.broadcasted_iota(jnp.int32, (dk, dk), 0) == lax.broadcasted_iota(jnp.int32, (dk, dk), 1)
    col = lambda rowv: jnp.sum(jnp.where(eye, jnp.broadcast_to(rowv, (dk, dk)), 0.0), axis=-1, keepdims=True)
    for hd in range(GLA_HEADS):
        ks = slice(hd * dk, (hd + 1) * dk)
        vs = slice(hd * GLA_DV, (hd + 1) * GLA_DV)
        s_new = jnp.exp(col(lf_ref[:, ks])) * s0_ref[hd] + col(k_ref[:, ks]) * v_ref[:, vs]
        s_ref[hd] = s_new
        o = jnp.sum(col(q_ref[:, ks]) * s_new, axis=0, keepdims=True)
        hc_ref[:, vs] = _rms(o, g_ref[...]) * sg_ref[:, vs]


def _gla_step(gq, gk, gv, lf, sg, g_gla, s0):
    b = gq.shape[0]
    r3 = lambda a: a.reshape(b, 1, a.shape[-1])
    blk = lambda n: pl.BlockSpec((None, 1, n), lambda i: (i, 0, 0))
    sblk = pl.BlockSpec((None, GLA_HEADS, GLA_DK, GLA_DV), lambda i: (i, 0, 0, 0))
    hc, s = pl.pallas_call(
        _gla_step_kernel,
        grid=(b,),
        in_specs=[blk(GLA_KW), blk(GLA_KW), blk(GLA_VW), blk(GLA_KW), blk(GLA_VW), _const_spec(g_gla.shape), sblk],
        out_specs=[blk(GLA_VW), sblk],
        out_shape=[jax.ShapeDtypeStruct((b, 1, GLA_VW), F32), jax.ShapeDtypeStruct(s0.shape, F32)],
        compiler_params=_params(("parallel",)),
        name="gla_step",
    )(r3(gq), r3(gk), r3(gv), r3(lf), r3(sg), g_gla, s0)
    return hc.reshape(b, GLA_VW), s


def kernel(x_prompt, x_sample, cache_ckv, cache_krope, state_gla, page_table, g_mix, w_in, g_q, w_uq, g_kv, w_uk, w_uv, g_v, b_v, w_s, b_s, w_a2, b_a, g_gla, w_pa, w_pb, w_pc, w_o, g_ffn, w_gu, w_down, g_final):
    batch, seq, _ = x_prompt.shape
    dec_b, dec_t, _ = x_sample.shape
    assert dec_t == 1 and seq % ROW_TILE == 0
    depth = w_in.shape[0]
    past = page_table.shape[1] * PAGE_SIZE
    ct_p, st_p = _rope_tables(jnp.arange(seq, dtype=F32))
    ct_s, st_s = _rope_tables(jnp.full((dec_b,), float(past), F32))
    gfin = g_final.reshape(1, -1).astype(F32)
    cache_krope_t = jnp.swapaxes(cache_krope, 2, 3)

    xp = x_prompt.reshape(batch * seq, D_MODEL)
    xs = x_sample.reshape(dec_b, D_MODEL)
    ckv_p, kr_p, gla_p, ckv_s, kr_s, gla_s, gv_s = [], [], [], [], [], [], []
    for l in range(depth):
        w = _prep_layer(l, g_mix, w_in, g_q, w_uq, g_kv, w_uk, w_uv, g_v, b_v, w_s, b_s, w_a2, b_a, g_gla,
                        w_pa, w_pb, w_pc, w_o, g_ffn, w_gu, w_down)
        last = l == depth - 1
        q, k, v, ckv, kr, hb, gq, gk, gv, lf, sg = _front(xp, ct_p, st_p, w, True, seq)
        ha = _attention(q, k, v, batch, seq)
        hc, s_c = _gla(gq, gk, gv, lf, sg, w['g_gla'], batch, seq)
        xp = _ffn(_merge(xp, ha, hb, hc, w, ROW_TILE), w, gfin, last, ROW_TILE)
        ckv_p.append(ckv.reshape(batch, seq, KV_RANK))
        kr_p.append(kr.reshape(batch, seq, ROPE_DIM))
        gla_p.append(s_c)
        q, ql, ckv, kr, hb, vn, gq, gk, gv, lf, sg = _front(xs, ct_s, st_s, w, False, seq)
        qr = q.reshape(dec_b, MLA_HEADS, HEAD_PAD)[:, :, NOPE_DIM:NOPE_DIM + ROPE_DIM]
        o_lat = _decode(l, page_table, ql.reshape(dec_b, MLA_HEADS, KV_RANK), qr,
                        ckv.reshape(dec_b, 1, KV_RANK), kr.reshape(dec_b, 1, ROPE_DIM), cache_ckv, cache_krope_t)
        ha = _sample_ha(o_lat.transpose(1, 0, 2), w['wuv_wide'])
        hc, s_c = _gla_step(gq, gk, gv, lf, sg, w['g_gla'], state_gla[l])
        xs = _ffn(_merge(xs, ha, hb, hc, w, dec_b), w, gfin, last, dec_b)
        ckv_s.append(ckv.reshape(dec_b, 1, KV_RANK))
        kr_s.append(kr.reshape(dec_b, 1, ROPE_DIM))
        gla_s.append(s_c)
        gv_s.append(vn.reshape(dec_b, 1, GMLP_WIDTH))
    return (xp.reshape(batch, seq, D_MODEL), xs.reshape(dec_b, 1, D_MODEL), jnp.stack(ckv_p), jnp.stack(kr_p),
            jnp.stack(gla_p), jnp.stack(ckv_s), jnp.stack(kr_s), jnp.stack(gla_s), jnp.stack(gv_s))
```

```python
import functools
import math

import numpy as np
import jax
import jax.numpy as jnp
from jax import lax
from jax.experimental import pallas as pl
from jax.experimental.pallas import tpu as pltpu

D_MODEL = 1024
PAGE_SIZE = 128
MLA_HEADS = 8
Q_RANK = 384
KV_RANK = 256
NOPE_DIM = 64
ROPE_DIM = 32
V_DIM = 64
ROPE_THETA = 10000.0
GMLP_GROUPS = 8
GMLP_CHUNK = 128
GMLP_WIDTH = 512
GLA_HEADS = 4
GLA_DK = 64
GLA_DV = 128
GLA_KW = GLA_HEADS * GLA_DK
GLA_VW = GLA_HEADS * GLA_DV
GLA_GATE_RANK = 16
GLA_TAU = 16.0
N_BRANCH = 3
D_FF = 2816
IN_SPLITS = (Q_RANK, KV_RANK, ROPE_DIM, GMLP_WIDTH, GMLP_WIDTH, GLA_KW, GLA_KW, GLA_VW,
             GLA_GATE_RANK, GLA_VW, N_BRANCH * D_MODEL)

LANE = 128
HEAD_PAD = 128
GLA_CHUNK = 128
ROW_TILE = 512
ATTN_TILE = 256
ATTN_Q_TILE = 512
FF_CHUNK = 256
VMEM_LIMIT = 56 * 1024 * 1024

BF = jnp.bfloat16
F32 = jnp.float32


def _dot(a, b):
    return jnp.dot(a, b, preferred_element_type=F32)


def _dot_nt(a, b):
    return lax.dot_general(a, b, (((1,), (1,)), ((), ())), preferred_element_type=F32)


def _dot_tn(a, b):
    return lax.dot_general(a, b, (((0,), (0,)), ((), ())), preferred_element_type=F32)


def _rms(x, g, eps=1e-6):
    return x * lax.rsqrt(jnp.mean(x * x, axis=-1, keepdims=True) + eps) * g


def _const_spec(shape):
    nd = len(shape)
    return pl.BlockSpec(shape, lambda *_: (0,) * nd, pipeline_mode=pl.Buffered(1))


def _layer_spec(arr, l):
    nd = arr.ndim - 1
    return pl.BlockSpec((None,) + arr.shape[1:], lambda *_: (l,) + (0,) * nd, pipeline_mode=pl.Buffered(1))


def _params(sem):
    return pltpu.CompilerParams(dimension_semantics=sem, vmem_limit_bytes=VMEM_LIMIT)


def _prep_weights(g_mix, w_in, g_q, w_uq, g_kv, w_uk, w_uv, g_v, b_v, w_s, b_s, w_a2, b_a, g_gla,
                  w_pa, w_pb, w_pc, w_o, g_ffn, w_gu, w_down):
    d = w_in.shape[0]
    o = np.cumsum((0,) + IN_SPLITS)
    cols = lambda i: w_in[:, :, o[i]:o[i + 1]]
    half = ROPE_DIM // 2
    kr = cols(2)
    kr_swap = jnp.concatenate([-kr[..., half:], kr[..., :half]], axis=-1)
    z16 = jnp.zeros((d, D_MODEL, 16), F32)
    z32 = jnp.zeros((d, D_MODEL, 32), F32)
    small = jnp.concatenate([kr_swap, cols(8), z16, kr, z32], axis=-1)
    wf = jnp.concatenate([cols(0), cols(1), small, cols(3), cols(4), cols(5), cols(6), cols(7), cols(9)],
                         axis=-1).astype(BF)
    wgate = cols(10).astype(BF)

    qn, qr = w_uq[..., :NOPE_DIM], w_uq[..., NOPE_DIM:]
    qr_swap = jnp.concatenate([-qr[..., half:], qr[..., :half]], axis=-1)
    wq_cat = jnp.concatenate([qn, qr, qr_swap], axis=-1).reshape(d, Q_RANK, MLA_HEADS * HEAD_PAD).astype(BF)

    wk_pad = jnp.concatenate([w_uk, jnp.zeros_like(w_uk)], axis=-1).reshape(d, KV_RANK, MLA_HEADS * HEAD_PAD).astype(BF)
    wuvt = w_uv.reshape(d, KV_RANK, MLA_HEADS * V_DIM).transpose(0, 2, 1).astype(BF)
    wukt = jnp.concatenate([w_uk.transpose(0, 2, 3, 1),
                            jnp.zeros((d, MLA_HEADS, HEAD_PAD - NOPE_DIM, KV_RANK), F32)], axis=2).astype(BF)
    eye_h = jnp.eye(MLA_HEADS, dtype=F32)
    wuv_wide = jnp.einsum('lrhd,hg->lhrgd', w_uv, eye_h).reshape(d, MLA_HEADS, KV_RANK, MLA_HEADS * V_DIM).astype(BF)

    wa2_pad = jnp.zeros((d, LANE, GLA_KW), F32).at[:, 32:32 + GLA_GATE_RANK].set(w_a2).astype(BF)
    tri = jnp.tril(jnp.ones((GMLP_CHUNK, GMLP_CHUNK), bool))
    ws = jnp.where(tri[None, None], w_s, 0.0).astype(BF)
    gd = GMLP_WIDTH // GMLP_GROUPS
    bias_tab = jnp.repeat(b_s.transpose(0, 2, 1), gd, axis=2)
    ws00 = jnp.repeat(w_s[:, :, 0, 0], gd, axis=1)[:, None, :]
    bs0 = jnp.repeat(b_s[:, :, 0], gd, axis=1)[:, None, :]

    nff = D_FF // FF_CHUNK
    wg = w_gu[:, :, :D_FF].reshape(d, D_MODEL, nff, FF_CHUNK).transpose(0, 2, 1, 3).astype(BF)
    wu = w_gu[:, :, D_FF:].reshape(d, D_MODEL, nff, FF_CHUNK).transpose(0, 2, 1, 3).astype(BF)
    wd = w_down.reshape(d, nff, FF_CHUNK, D_MODEL).astype(BF)
    row = lambda a: a.reshape(d, 1, -1).astype(F32)
    return dict(
        wf=wf, wgate=wgate, wq_cat=wq_cat, wk_pad=wk_pad, wuvt=wuvt, wukt=wukt,
        wuv_wide=wuv_wide, wa2_pad=wa2_pad, ws=ws, bias_tab=bias_tab, ws00=ws00, bs0=bs0,
        g_mix=row(g_mix), g_q=row(g_q), g_kv=row(g_kv), g_v=row(g_v), b_v=row(b_v),
        b_a=row(b_a), g_gla=row(g_gla), g_ffn=row(g_ffn),
        wpa=w_pa.astype(BF), wpb=w_pb.astype(BF), wpc=w_pc.astype(BF), wo=w_o.astype(BF),
        wg=wg, wu=wu, wd=wd)


def _rope_tables(pos):
    half = ROPE_DIM // 2
    inv = jnp.power(ROPE_THETA, -jnp.arange(half, dtype=F32) * 2.0 / ROPE_DIM)
    ang = pos[:, None] * inv[None, :]
    cos, sin = jnp.cos(ang), jnp.sin(ang)
    t = pos.shape[0]
    ctab = jnp.concatenate([jnp.ones((t, NOPE_DIM), F32), cos, cos, jnp.zeros((t, 32), F32)], axis=1)
    stab = jnp.concatenate([jnp.zeros((t, NOPE_DIM), F32), sin, sin, jnp.zeros((t, 32), F32)], axis=1)
    return ctab, stab


_WF_OFF = np.cumsum((0, Q_RANK, KV_RANK, LANE, GMLP_WIDTH, GMLP_WIDTH, GLA_KW, GLA_KW, GLA_VW, GLA_VW))


def _front_kernel(prompt, tm, x_ref, ct_ref, st_ref, wf_ref, wq_cat_ref, wkv_a_ref, wkv_b_ref,
                  wa2_ref, gm_a_ref, gm_b_ref, gmix_ref, gq_ref, gkv_ref, gv_ref, bv_ref, ba_ref, *outs):
    if prompt:
        (q_ref, k_ref, v_ref, ckv_ref, kr_ref, hb_ref, oq_ref, ok_ref, ov_ref, lf_ref, sg_ref) = outs
    else:
        (q_ref, ql_ref, ckv_ref, kr_ref, hb_ref, vn_ref, oq_ref, ok_ref, ov_ref, lf_ref, sg_ref) = outs
    wcol = lambda i: wf_ref[:, _WF_OFF[i]:_WF_OFF[i + 1]]
    x = x_ref[...]
    h = _rms(x, gmix_ref[...]).astype(BF)
    ct = ct_ref[...]
    st = st_ref[...]
    scale = (NOPE_DIM + ROPE_DIM) ** -0.5 * math.log2(math.e)

    cq = _rms(_dot(h, wcol(0)), gq_ref[...]).astype(BF)
    qa = _dot(cq, wq_cat_ref[...])
    for hd in range(MLA_HEADS):
        sl = slice(hd * HEAD_PAD, (hd + 1) * HEAD_PAD)
        qh = (qa[:, sl] * ct + pltpu.roll(qa[:, sl], HEAD_PAD - ROPE_DIM, 1) * st) * scale
        q_ref[:, sl] = qh.astype(q_ref.dtype)
        if not prompt:
            ql_ref[:, hd * KV_RANK:(hd + 1) * KV_RANK] = _dot(qh.astype(BF), wkv_a_ref[hd])

    ckv = _rms(_dot(h, wcol(1)), gkv_ref[...])
    ckv_ref[...] = ckv
    small = _dot(h, wcol(2))
    lane = lax.broadcasted_iota(jnp.int32, small.shape, 1)
    kr_rot = jnp.where((lane >= NOPE_DIM) & (lane < NOPE_DIM + ROPE_DIM),
                       small * ct + pltpu.roll(small, NOPE_DIM, 1) * st, 0.0)
    kr_ref[...] = pltpu.roll(kr_rot, NOPE_DIM, 1)[:, :ROPE_DIM]
    if prompt:
        ckvb = ckv.astype(BF)
        kn = _dot(ckvb, wkv_a_ref[...])
        for hd in range(MLA_HEADS):
            sl = slice(hd * HEAD_PAD, (hd + 1) * HEAD_PAD)
            k_ref[:, sl] = (kn[:, sl] + kr_rot).astype(BF)
        vt = _dot_nt(wkv_b_ref[...], ckvb).astype(BF)
        ones = jnp.ones((V_DIM, ATTN_TILE), BF)
        for t in range(tm // ATTN_TILE):
            ts = slice(t * ATTN_TILE, (t + 1) * ATTN_TILE)
            for hd in range(MLA_HEADS):
                val = hd * HEAD_PAD + (hd % 2) * V_DIM
                pad = hd * HEAD_PAD + (1 - hd % 2) * V_DIM
                v_ref[t, val:val + V_DIM, :] = vt[hd * V_DIM:(hd + 1) * V_DIM, ts]
                v_ref[t, pad:pad + V_DIM, :] = ones

    u = _dot(h, wcol(3))
    v = _dot(h, wcol(4))
    mu = jnp.mean(v, axis=-1, keepdims=True)
    vc = v - mu
    var = jnp.mean(vc * vc, axis=-1, keepdims=True)
    vn = vc * lax.rsqrt(var + 1e-5) * gv_ref[...] + bv_ref[...]
    if prompt:
        vnb = vn.astype(BF)
        lane2 = lax.broadcasted_iota(jnp.int32, (GMLP_CHUNK, LANE), 1)
        gd = GMLP_WIDTH // GMLP_GROUPS
        for c in range(tm // GMLP_CHUNK):
            rs = slice(c * GMLP_CHUNK, (c + 1) * GMLP_CHUNK)
            for gp in range(GMLP_GROUPS // 2):
                cs = slice(gp * LANE, (gp + 1) * LANE)
                vp = vnb[rs, cs]
                mixed = jnp.where(lane2 < gd, _dot(gm_a_ref[2 * gp], vp), _dot(gm_a_ref[2 * gp + 1], vp))
                hb_ref[rs, cs] = (u[rs, cs] * (mixed + gm_b_ref[:, cs])).astype(hb_ref.dtype)
    else:
        vn_ref[...] = vn
        hb_ref[...] = (u * (gm_a_ref[...] * vn + gm_b_ref[...])).astype(hb_ref.dtype)

    oq_ref[...] = _dot(h, wcol(5)) * (GLA_DK ** -0.5)
    ok_ref[...] = _dot(h, wcol(6))
    ov_ref[...] = _dot(h, wcol(7)).astype(ov_ref.dtype)
    a = _dot(small.astype(BF), wa2_ref[...]) + ba_ref[...]
    lf_ref[...] = (jnp.minimum(a, 0.0) - jnp.log(1.0 + jnp.exp(-jnp.abs(a)))) * (1.0 / GLA_TAU)
    gg = _dot(h, wcol(8))
    sg_ref[...] = (gg * jax.nn.sigmoid(gg)).astype(sg_ref.dtype)


def _front(x, ctab, stab, w, l, prompt, seq):
    r = x.shape[0]
    tm = ROW_TILE if prompt else r
    nt = r // tm
    row = lambda n: pl.BlockSpec((tm, n), lambda i: (i, 0))
    rows = lambda n, dt: (row(n), jax.ShapeDtypeStruct((r, n), dt))
    width = MLA_HEADS * HEAD_PAD
    if prompt:
        tab = pl.BlockSpec((tm, LANE), lambda i: (i % (seq // tm), 0))
        wkv_a, wkv_b = w['wk_pad'], w['wuvt']
        gm_a, gm_b = w['ws'], w['bias_tab']
        vt_out = (pl.BlockSpec((tm // ATTN_TILE, width, ATTN_TILE), lambda i: (i, 0, 0)),
                  jax.ShapeDtypeStruct((r // ATTN_TILE, width, ATTN_TILE), BF))
        outs = (rows(width, BF), rows(width, BF), vt_out, rows(KV_RANK, F32), rows(ROPE_DIM, F32),
                rows(GMLP_WIDTH, BF), rows(GLA_KW, F32), rows(GLA_KW, F32), rows(GLA_VW, BF), rows(GLA_KW, F32),
                rows(GLA_VW, BF))
    else:
        tab = pl.BlockSpec((tm, LANE), lambda i: (0, 0))
        wkv_a, wkv_b = w['wukt'], w['wuvt']
        gm_a, gm_b = w['ws00'], w['bs0']
        outs = (rows(width, F32), rows(MLA_HEADS * KV_RANK, F32), rows(KV_RANK, F32), rows(ROPE_DIM, F32),
                rows(GMLP_WIDTH, BF), rows(GMLP_WIDTH, F32), rows(GLA_KW, F32), rows(GLA_KW, F32), rows(GLA_VW, F32),
                rows(GLA_KW, F32), rows(GLA_VW, F32))
    consts = (w['wf'], w['wq_cat'], wkv_a, wkv_b, w['wa2_pad'], gm_a, gm_b,
              w['g_mix'], w['g_q'], w['g_kv'], w['g_v'], w['b_v'], w['b_a'])
    return pl.pallas_call(
        functools.partial(_front_kernel, prompt, tm),
        grid=(nt,),
        in_specs=[row(D_MODEL), tab, tab] + [_layer_spec(c, l) for c in consts],
        out_specs=[o[0] for o in outs],
        out_shape=[o[1] for o in outs],
        compiler_params=_params(("parallel",)),
        name="front_prompt" if prompt else "front_sample",
    )(x, ctab, stab, *consts)


def _attn_kernel(q_ref, k_ref, vt_ref, o_ref, m_ref, acc_ref, st_ref, p_ref):
    tk, tq = ATTN_TILE, ATTN_Q_TILE
    ratio = tq // tk
    qi = pl.program_id(1)
    m_ref[...] = jnp.full(m_ref.shape, -jnp.inf, F32)
    acc_ref[...] = jnp.zeros(acc_ref.shape, F32)
    krow = lax.broadcasted_iota(jnp.int32, (tk, tq), 0)
    qcol = lax.broadcasted_iota(jnp.int32, (tk, tq), 1)

    def tile(j, diag):
        off = pl.multiple_of(j * tk, tk)
        heads = [slice(hd * HEAD_PAD, (hd + 1) * HEAD_PAD) for hd in range(MLA_HEADS)]
        mx = []
        for hd, cs in enumerate(heads):
            st = _dot_nt(k_ref[pl.ds(off, tk), cs], q_ref[:, cs])
            if diag is not None:
                st = jnp.where(krow + diag * tk <= qcol, st, -jnp.inf)
            st_ref[hd] = st
            mx.append(jnp.max(st, axis=0, keepdims=True))
        alpha = []
        for hd, cs in enumerate(heads):
            m = m_ref[hd]
            m_new = jnp.maximum(m, mx[hd])
            alpha.append(jnp.exp2(m - m_new))
            m_ref[hd] = m_new
            p_ref[hd] = jnp.exp2(st_ref[hd] - m_new).astype(BF)
        for hd, cs in enumerate(heads):
            acc_ref[hd] = alpha[hd] * acc_ref[hd] + _dot(vt_ref[j, cs, :], p_ref[hd])

    def body(j, carry):
        tile(j, None)
        return carry

    lax.fori_loop(0, qi * ratio, body, 0)
    for d in range(ratio):
        tile(qi * ratio + d, d)
    rowi = lax.broadcasted_iota(jnp.int32, (HEAD_PAD, tq), 0)
    for pr in range(MLA_HEADS // 2):
        ae = acc_ref[2 * pr]
        ao = acc_ref[2 * pr + 1]
        out_t = jnp.where(rowi < V_DIM, ae / ae[V_DIM:V_DIM + 1, :], ao / ao[0:1, :])
        o_ref[:, pr * HEAD_PAD:(pr + 1) * HEAD_PAD] = out_t.T.astype(o_ref.dtype)


def _attention(q, k, vt, batch, seq):
    tk, tq = ATTN_TILE, ATTN_Q_TILE
    width = MLA_HEADS * HEAD_PAD
    q3, k3 = (a.reshape(batch, seq, width) for a in (q, k))
    vt4 = vt.reshape(batch, seq // tk, width, tk)
    out = pl.pallas_call(
        _attn_kernel,
        grid=(batch, seq // tq),
        in_specs=[pl.BlockSpec((None, tq, width), lambda b, i: (b, i, 0)),
                  pl.BlockSpec((None, seq, width), lambda b, i: (b, 0, 0)),
                  pl.BlockSpec((None, seq // tk, width, tk), lambda b, i: (b, 0, 0, 0))],
        out_specs=pl.BlockSpec((None, tq, MLA_HEADS * V_DIM), lambda b, i: (b, i, 0)),
        out_shape=jax.ShapeDtypeStruct((batch, seq, MLA_HEADS * V_DIM), BF),
        scratch_shapes=[pltpu.VMEM((MLA_HEADS, 1, tq), F32), pltpu.VMEM((MLA_HEADS, HEAD_PAD, tq), F32),
                        pltpu.VMEM((MLA_HEADS, tk, tq), F32), pltpu.VMEM((MLA_HEADS, tk, tq), BF)],
        compiler_params=_params(("parallel", "arbitrary")),
        name="mla_prompt_attn",
    )(q3, k3, vt4)
    return out.reshape(batch * seq, MLA_HEADS * V_DIM)


_GLA_LEVELS = (1, 2, 4, 8, 16, 32, 64)


def _gla_tables():
    c = GLA_CHUNK
    t = np.arange(c)[:, None]
    r = np.arange(c)[None, :]
    blocks = []
    for m in _GLA_LEVELS[1:] + (c,):
        blocks.append(((t // m == r // m) & (r <= t)))
    for m in _GLA_LEVELS[1:] + (c,):
        blocks.append(((t // m == r // m) & (r > t)))
    lstack = np.concatenate(blocks, axis=0).astype(np.float32)
    lstack = np.concatenate([lstack, lstack], axis=1)
    masks = [np.eye(c, dtype=np.float32)]
    for m in _GLA_LEVELS:
        masks.append((((t // m) % 2 == 1) & (r // m == t // m - 1)).astype(np.float32))
    return jnp.asarray(lstack, BF), jnp.asarray(np.stack(masks), F32)


def _gla_kernel(nchunk, q_ref, k_ref, v_ref, lf_ref, sg_ref, l_ref, mask_ref, g_ref, hc_ref, s_ref,
                px_ref, qt_ref, kt_ref, z_ref):
    c = GLA_CHUNK
    ci = pl.program_id(1)
    nlev = len(_GLA_LEVELS)
    npair = GLA_HEADS // 2

    @pl.when(ci == 0)
    def _():
        z_ref[...] = jnp.zeros_like(z_ref)

    lf = lf_ref[...]
    hi = lf.astype(BF)
    lo = (lf - hi.astype(F32)).astype(BF)
    px_ref[...] = _dot(l_ref[...], jnp.concatenate([hi, lo], axis=0))
    q = q_ref[...]
    k = k_ref[...]
    lane = lax.broadcasted_iota(jnp.int32, (c, LANE), 1)
    low = lane < GLA_DK

    def put(idx, qv, kv):
        for p in range(npair):
            qp = qv[:, p * LANE:(p + 1) * LANE]
            qt_ref[idx, p, 0:c, :] = jnp.where(low, qp, 0.0).astype(BF)
            qt_ref[idx, p, c:2 * c, :] = jnp.where(low, 0.0, qp).astype(BF)
        kt_ref[idx] = kv.astype(BF)

    pblk = lambda i: px_ref[i * c:(i + 1) * c, :]
    put(0, q, k)
    put(1, q * jnp.exp(lf), k)
    for i in range(nlev - 1):
        put(2 + i, q * jnp.exp(pblk(i)), k * jnp.exp(pblk(nlev + i)))
    bcum = pblk(nlev - 1)
    put(nlev + 1, q * jnp.exp(bcum), k * jnp.exp(pblk(2 * nlev - 1)))
    decay = jnp.exp(bcum[c - 1:c, :])

    for p in range(npair):
        a = jnp.zeros((2 * c, c), F32)
        for lv in range(nlev + 1):
            mk = mask_ref[lv]
            sc = _dot_nt(qt_ref[lv, p], kt_ref[lv, :, p * LANE:(p + 1) * LANE])
            a = a + sc * jnp.concatenate([mk, mk], axis=0)
        kx = kt_ref[nlev + 1, :, p * LANE:(p + 1) * LANE]
        for hh in range(2):
            hd = 2 * p + hh
            vs = slice(hd * GLA_DV, (hd + 1) * GLA_DV)
            vh = v_ref[:, vs]
            z = z_ref[hd]
            o = (_dot(a[hh * c:(hh + 1) * c].astype(BF), vh)
                 + _dot_nt(qt_ref[nlev + 1, p, hh * c:(hh + 1) * c, :], z.astype(BF)))
            y = _rms(o, g_ref[...])
            hc_ref[:, vs] = (y * sg_ref[:, vs].astype(F32)).astype(hc_ref.dtype)
            z_ref[hd] = z * decay[:, p * LANE:(p + 1) * LANE] + _dot_tn(vh, kx)

    @pl.when(ci == nchunk - 1)
    def _():
        for hd in range(GLA_HEADS):
            zt = z_ref[hd].T
            s_ref[hd] = zt[(hd % 2) * GLA_DK:(hd % 2 + 1) * GLA_DK, :]


def _gla(gq, gk, gv, lf, sg, g_gla, l, batch, seq):
    c = GLA_CHUNK
    nchunk = seq // c
    lstack, masks = _gla_tables()
    nlev = len(_GLA_LEVELS)
    r3 = lambda a: a.reshape(batch, seq, a.shape[-1])
    blk = lambda n: pl.BlockSpec((None, c, n), lambda b, i: (b, i, 0))
    hc, s = pl.pallas_call(
        functools.partial(_gla_kernel, nchunk),
        grid=(batch, nchunk),
        in_specs=[blk(GLA_KW), blk(GLA_KW), blk(GLA_VW), blk(GLA_KW), blk(GLA_VW),
                  _const_spec(lstack.shape), _const_spec(masks.shape), _layer_spec(g_gla, l)],
        out_specs=[blk(GLA_VW), pl.BlockSpec((None, GLA_HEADS, GLA_DK, GLA_DV), lambda b, i: (b, 0, 0, 0))],
        out_shape=[jax.ShapeDtypeStruct((batch, seq, GLA_VW), BF),
                   jax.ShapeDtypeStruct((batch, GLA_HEADS, GLA_DK, GLA_DV), F32)],
        scratch_shapes=[pltpu.VMEM((2 * nlev * c, GLA_KW), F32),
                        pltpu.VMEM((nlev + 2, GLA_HEADS // 2, 2 * c, LANE), BF),
                        pltpu.VMEM((nlev + 2, c, GLA_KW), BF),
                        pltpu.VMEM((GLA_HEADS, GLA_DV, LANE), F32)],
        compiler_params=_params(("parallel", "arbitrary")),
        name="gla_prompt",
    )(r3(gq), r3(gk), r3(gv), r3(lf), r3(sg), lstack, masks, g_gla)
    return hc.reshape(batch * seq, GLA_VW), s


def _merge_kernel(x_ref, ha_ref, hb_ref, hc_ref, gmix_ref, wgate_ref, wpa_ref, wpb_ref, wpc_ref, wo_ref, o_ref):
    x = x_ref[...]
    h = _rms(x, gmix_ref[...]).astype(BF)
    mix = None
    for j, (hr, wr) in enumerate(((ha_ref, wpa_ref), (hb_ref, wpb_ref), (hc_ref, wpc_ref))):
        gate = jax.nn.sigmoid(_dot(h, wgate_ref[:, j * D_MODEL:(j + 1) * D_MODEL]))
        term = gate * _dot(hr[...].astype(BF), wr[...])
        mix = term if mix is None else mix + term
    o_ref[...] = x + _dot(mix.astype(BF), wo_ref[...])


def _merge(x, ha, hb, hc, w, l, tm):
    r = x.shape[0]
    row = lambda n: pl.BlockSpec((tm, n), lambda i: (i, 0))
    consts = (w['g_mix'], w['wgate'], w['wpa'], w['wpb'], w['wpc'], w['wo'])
    return pl.pallas_call(
        _merge_kernel,
        grid=(r // tm,),
        in_specs=[row(D_MODEL), row(ha.shape[1]), row(hb.shape[1]), row(hc.shape[1])]
                 + [_layer_spec(c, l) for c in consts],
        out_specs=row(D_MODEL),
        out_shape=jax.ShapeDtypeStruct((r, D_MODEL), F32),
        compiler_params=_params(("parallel",)),
        name="merge",
    )(x, ha, hb, hc, *consts)


def _ffn_kernel(final, x_ref, g_ref, wg_ref, wu_ref, wd_ref, gf_ref, o_ref):
    x = x_ref[...]
    hn = _rms(x, g_ref[...]).astype(BF)
    acc = jnp.zeros(x.shape, F32)
    for cidx in range(D_FF // FF_CHUNK):
        g = _dot(hn, wg_ref[cidx])
        u = _dot(hn, wu_ref[cidx])
        act = (g * jax.nn.sigmoid(g) * u).astype(BF)
        acc = acc + _dot(act, wd_ref[cidx])
    y = x + acc
    if final:
        y = _rms(y, gf_ref[...])
    o_ref[...] = y


def _ffn(x, w, l, g_final, final, tm):
    r = x.shape[0]
    row = pl.BlockSpec((tm, D_MODEL), lambda i: (i, 0))
    consts = (w['g_ffn'], w['wg'], w['wu'], w['wd'], g_final)
    return pl.pallas_call(
        functools.partial(_ffn_kernel, final),
        grid=(r // tm,),
        in_specs=[row] + [_layer_spec(c, l) for c in consts[:-1]] + [_const_spec(g_final.shape)],
        out_specs=row,
        out_shape=jax.ShapeDtypeStruct((r, D_MODEL), F32),
        compiler_params=_params(("parallel",)),
        name="ffn",
    )(x, *consts)


DECODE_PAGES = 32
PAGE_GROUP = 8


def _decode_kernel(layer, nch, pt_ref, ql_ref, qr_ref, cn_ref, kn_ref, ckv_hbm, krt_hbm, o_ref,
                   cbuf, rbuf, cb, rb, sem, m_ref, l_ref, acc_ref):
    npg = DECODE_PAGES
    total = ql_ref.shape[0] * nch

    def page_copies(slot, i, pid):
        rows = pl.ds(i * PAGE_SIZE, PAGE_SIZE)
        return (pltpu.make_async_copy(ckv_hbm.at[layer, pid], cbuf.at[slot, rows, :], sem.at[0, slot]),
                pltpu.make_async_copy(krt_hbm.at[layer, pid], rbuf.at[slot, :, rows], sem.at[1, slot]))

    def start(step, slot):
        b = step // nch
        c = step % nch
        for i in range(npg):
            for cp in page_copies(slot, i, pt_ref[b, c * npg + i]):
                cp.start()

    def wait(slot):
        for i in range(npg):
            for cp in page_copies(slot, i, 0):
                cp.wait()

    start(0, 0)

    def body(step, carry):
        slot = step % 2
        b = step // nch
        c = step % nch

        @pl.when(step + 1 < total)
        def _():
            start(step + 1, 1 - slot)

        ql = ql_ref[b]
        qr = qr_ref[b]

        @pl.when(c == 0)
        def _():
            cn = cn_ref[b]
            m_ref[...] = (jnp.sum(ql * cn, axis=-1, keepdims=True) + jnp.sum(qr * kn_ref[b], axis=-1, keepdims=True))
            l_ref[...] = jnp.ones_like(l_ref)
            acc_ref[...] = jnp.broadcast_to(cn, acc_ref.shape)

        wait(slot)
        qlb = ql.astype(BF)
        qrb = qr.astype(BF)
        span = PAGE_GROUP * PAGE_SIZE
        ss = []
        for g in range(npg // PAGE_GROUP):
            rows = slice(g * span, (g + 1) * span)
            cb[rows, :] = cbuf[slot, rows, :].astype(BF)
            rb[:, rows] = rbuf[slot, :, rows].astype(BF)
            ss.append(_dot_nt(qlb, cb[rows, :]) + _dot(qrb, rb[:, rows]))
        mc = ss[0].max(axis=-1, keepdims=True)
        for s in ss[1:]:
            mc = jnp.maximum(mc, s.max(axis=-1, keepdims=True))
        m = m_ref[...]
        m_new = jnp.maximum(m, mc)
        alpha = jnp.exp2(m - m_new)
        l = alpha * l_ref[...]
        acc = alpha * acc_ref[...]
        for g, s in enumerate(ss):
            p = jnp.exp2(s - m_new)
            l = l + jnp.sum(p, axis=-1, keepdims=True)
            acc = acc + _dot(p.astype(BF), cb[g * span:(g + 1) * span, :])
        m_ref[...] = m_new
        l_ref[...] = l
        acc_ref[...] = acc

        @pl.when(c == nch - 1)
        def _():
            o_ref[b] = acc / l

        return carry

    lax.fori_loop(0, total, body, 0)


def _decode(layer, page_table, ql, qr, ckv_new, kr_new, cache_ckv, cache_krope_t):
    b, npages = page_table.shape
    npg = DECODE_PAGES
    assert npages % npg == 0 and npg % PAGE_GROUP == 0
    keys = npg * PAGE_SIZE
    vmem = pl.BlockSpec(memory_space=pltpu.VMEM)
    hbm = pl.BlockSpec(memory_space=pl.ANY)
    return pl.pallas_call(
        functools.partial(_decode_kernel, layer, npages // npg),
        in_specs=[pl.BlockSpec(memory_space=pltpu.SMEM), vmem, vmem, vmem, vmem, hbm, hbm],
        out_specs=vmem,
        out_shape=jax.ShapeDtypeStruct((b, MLA_HEADS, KV_RANK), F32),
        scratch_shapes=[pltpu.VMEM((2, keys, KV_RANK), F32), pltpu.VMEM((2, ROPE_DIM, keys), F32),
                        pltpu.VMEM((keys, KV_RANK), BF), pltpu.VMEM((ROPE_DIM, keys), BF),
                        pltpu.SemaphoreType.DMA((2, 2)),
                        pltpu.VMEM((MLA_HEADS, 1), F32), pltpu.VMEM((MLA_HEADS, 1), F32),
                        pltpu.VMEM((MLA_HEADS, KV_RANK), F32)],
        compiler_params=pltpu.CompilerParams(vmem_limit_bytes=VMEM_LIMIT),
        name="mla_decode",
    )(page_table, ql, qr, ckv_new, kr_new, cache_ckv, cache_krope_t)


def _sample_ha_kernel(ol_ref, w_ref, o_ref):
    acc = None
    for hd in range(MLA_HEADS):
        t = _dot(ol_ref[hd].astype(BF), w_ref[hd])
        acc = t if acc is None else acc + t
    o_ref[...] = acc


def _sample_ha(o_lat_t, wuv_wide, l):
    b = o_lat_t.shape[1]
    return pl.pallas_call(
        _sample_ha_kernel,
        grid=(1,),
        in_specs=[_const_spec(o_lat_t.shape), _layer_spec(wuv_wide, l)],
        out_specs=pl.BlockSpec((b, MLA_HEADS * V_DIM), lambda i: (0, 0)),
        out_shape=jax.ShapeDtypeStruct((b, MLA_HEADS * V_DIM), F32),
        compiler_params=_params(("arbitrary",)),
        name="sample_ha",
    )(o_lat_t, wuv_wide)


def _gla_step_kernel(q_ref, k_ref, v_ref, lf_ref, sg_ref, g_ref, s0_ref, hc_ref, s_ref):
    dk = GLA_DK
    eye = lax.broadcasted_iota(jnp.int32, (dk, dk), 0) == lax.broadcasted_iota(jnp.int32, (dk, dk), 1)
    col = lambda rowv: jnp.sum(jnp.where(eye, jnp.broadcast_to(rowv, (dk, dk)), 0.0), axis=-1, keepdims=True)
    for hd in range(GLA_HEADS):
        ks = slice(hd * dk, (hd + 1) * dk)
        vs = slice(hd * GLA_DV, (hd + 1) * GLA_DV)
        s_new = jnp.exp(col(lf_ref[:, ks])) * s0_ref[hd] + col(k_ref[:, ks]) * v_ref[:, vs]
        s_ref[hd] = s_new
        o = jnp.sum(col(q_ref[:, ks]) * s_new, axis=0, keepdims=True)
        hc_ref[:, vs] = _rms(o, g_ref[...]) * sg_ref[:, vs]


def _gla_step(gq, gk, gv, lf, sg, g_gla, state_all, l):
    b = gq.shape[0]
    r3 = lambda a: a.reshape(b, 1, a.shape[-1])
    blk = lambda n: pl.BlockSpec((None, 1, n), lambda i: (i, 0, 0))
    sshape = (GLA_HEADS, GLA_DK, GLA_DV)
    hc, s = pl.pallas_call(
        _gla_step_kernel,
        grid=(b,),
        in_specs=[blk(GLA_KW), blk(GLA_KW), blk(GLA_VW), blk(GLA_KW), blk(GLA_VW), _layer_spec(g_gla, l),
                  pl.BlockSpec((None, None) + sshape, lambda i: (l, i, 0, 0, 0))],
        out_specs=[blk(GLA_VW), pl.BlockSpec((None,) + sshape, lambda i: (i, 0, 0, 0))],
        out_shape=[jax.ShapeDtypeStruct((b, 1, GLA_VW), F32), jax.ShapeDtypeStruct((b,) + sshape, F32)],
        compiler_params=_params(("parallel",)),
        name="gla_step",
    )(r3(gq), r3(gk), r3(gv), r3(lf), r3(sg), g_gla, state_all)
    return hc.reshape(b, GLA_VW), s


def kernel(x_prompt, x_sample, cache_ckv, cache_krope, state_gla, page_table, g_mix, w_in, g_q, w_uq, g_kv, w_uk, w_uv, g_v, b_v, w_s, b_s, w_a2, b_a, g_gla, w_pa, w_pb, w_pc, w_o, g_ffn, w_gu, w_down, g_final):
    batch, seq, _ = x_prompt.shape
    dec_b, dec_t, _ = x_sample.shape
    assert dec_t == 1 and seq % ROW_TILE == 0
    depth = w_in.shape[0]
    past = page_table.shape[1] * PAGE_SIZE
    ct_p, st_p = _rope_tables(jnp.arange(seq, dtype=F32))
    ct_s, st_s = _rope_tables(jnp.full((dec_b,), float(past), F32))
    gfin = g_final.reshape(1, -1).astype(F32)
    cache_krope_t = jnp.swapaxes(cache_krope, 2, 3)

    xp = x_prompt.reshape(batch * seq, D_MODEL)
    xs = x_sample.reshape(dec_b, D_MODEL)
    w = _prep_weights(g_mix, w_in, g_q, w_uq, g_kv, w_uk, w_uv, g_v, b_v, w_s, b_s, w_a2, b_a, g_gla,
                      w_pa, w_pb, w_pc, w_o, g_ffn, w_gu, w_down)
    ckv_p, kr_p, gla_p, ckv_s, kr_s, gla_s, gv_s = [], [], [], [], [], [], []
    for l in range(depth):
        last = l == depth - 1
        q, k, v, ckv, kr, hb, gq, gk, gv, lf, sg = _front(xp, ct_p, st_p, w, l, True, seq)
        ha = _attention(q, k, v, batch, seq)
        hc, s_c = _gla(gq, gk, gv, lf, sg, w['g_gla'], l, batch, seq)
        xp = _ffn(_merge(xp, ha, hb, hc, w, l, ROW_TILE), w, l, gfin, last, ROW_TILE)
        ckv_p.append(ckv.reshape(batch, seq, KV_RANK))
        kr_p.append(kr.reshape(batch, seq, ROPE_DIM))
        gla_p.append(s_c)
        q, ql, ckv, kr, hb, vn, gq, gk, gv, lf, sg = _front(xs, ct_s, st_s, w, l, False, seq)
        qr = q.reshape(dec_b, MLA_HEADS, HEAD_PAD)[:, :, NOPE_DIM:NOPE_DIM + ROPE_DIM]
        o_lat = _decode(l, page_table, ql.reshape(dec_b, MLA_HEADS, KV_RANK), qr,
                        ckv.reshape(dec_b, 1, KV_RANK), kr.reshape(dec_b, 1, ROPE_DIM), cache_ckv, cache_krope_t)
        ha = _sample_ha(o_lat.transpose(1, 0, 2), w['wuv_wide'], l)
        hc, s_c = _gla_step(gq, gk, gv, lf, sg, w['g_gla'], state_gla, l)
        xs = _ffn(_merge(xs, ha, hb, hc, w, l, dec_b), w, l, gfin, last, dec_b)
        ckv_s.append(ckv.reshape(dec_b, 1, KV_RANK))
        kr_s.append(kr.reshape(dec_b, 1, ROPE_DIM))
        gla_s.append(s_c)
        gv_s.append(vn.reshape(dec_b, 1, GMLP_WIDTH))
    return (xp.reshape(batch, seq, D_MODEL), xs.reshape(dec_b, 1, D_MODEL), jnp.stack(ckv_p), jnp.stack(kr_p),
            jnp.stack(gla_p), jnp.stack(ckv_s), jnp.stack(kr_s), jnp.stack(gla_s), jnp.stack(gv_s))
```

```python
import functools
import math

import numpy as np
import jax
import jax.numpy as jnp
from jax import lax
from jax.experimental import pallas as pl
from jax.experimental.pallas import tpu as pltpu

D_MODEL = 1024
PAGE_SIZE = 128
MLA_HEADS = 8
Q_RANK = 384
KV_RANK = 256
NOPE_DIM = 64
ROPE_DIM = 32
V_DIM = 64
ROPE_THETA = 10000.0
GMLP_GROUPS = 8
GMLP_CHUNK = 128
GMLP_WIDTH = 512
GLA_HEADS = 4
GLA_DK = 64
GLA_DV = 128
GLA_KW = GLA_HEADS * GLA_DK
GLA_VW = GLA_HEADS * GLA_DV
GLA_GATE_RANK = 16
GLA_TAU = 16.0
N_BRANCH = 3
D_FF = 2816
IN_SPLITS = (Q_RANK, KV_RANK, ROPE_DIM, GMLP_WIDTH, GMLP_WIDTH, GLA_KW, GLA_KW, GLA_VW,
             GLA_GATE_RANK, GLA_VW, N_BRANCH * D_MODEL)
_IN_OFF = tuple(int(v) for v in np.cumsum((0,) + IN_SPLITS))

LANE = 128
HEAD_PAD = 128
GLA_CHUNK = 128
ROW_TILE = 512
ATTN_TILE = 256
ATTN_Q_TILE = 512
ATTN_HEAD_BATCH = 8
ATTN_ONES_ROWS = 16
FF_CHUNK = 256
VMEM_LIMIT = 56 * 1024 * 1024

BF = jnp.bfloat16
F32 = jnp.float32


def _dot(a, b):
    return jnp.dot(a, b, preferred_element_type=F32)


def _dot_nt(a, b):
    return lax.dot_general(a, b, (((1,), (1,)), ((), ())), preferred_element_type=F32)


def _dot_tn(a, b):
    return lax.dot_general(a, b, (((0,), (0,)), ((), ())), preferred_element_type=F32)


def _rms(x, g, eps=1e-6):
    return x * lax.rsqrt(jnp.mean(x * x, axis=-1, keepdims=True) + eps) * g


def _const_spec(shape):
    nd = len(shape)
    return pl.BlockSpec(shape, lambda *_: (0,) * nd, pipeline_mode=pl.Buffered(1))


def _layer_spec(arr, l):
    nd = arr.ndim - 1
    return pl.BlockSpec((None,) + arr.shape[1:], lambda *_: (l,) + (0,) * nd, pipeline_mode=pl.Buffered(1))


def _params(sem):
    return pltpu.CompilerParams(dimension_semantics=sem, vmem_limit_bytes=VMEM_LIMIT)


def _prep_weights(g_mix, w_in, g_q, w_uq, g_kv, w_uk, w_uv, g_v, b_v, w_s, b_s, w_a2, b_a, g_gla,
                  w_pa, w_pb, w_pc, w_o, g_ffn, w_gu, w_down):
    d = w_in.shape[0]
    half = ROPE_DIM // 2
    wt = jnp.swapaxes(w_in, 1, 2).astype(BF)
    kr = wt[:, _IN_OFF[2]:_IN_OFF[3]]
    kr_swap = jnp.concatenate([-kr[:, half:], kr[:, :half]], axis=1)
    z16 = jnp.zeros((d, 16, D_MODEL), BF)
    z32 = jnp.zeros((d, 32, D_MODEL), BF)
    small_t = jnp.concatenate([kr_swap, wt[:, _IN_OFF[8]:_IN_OFF[9]], z16, kr, z32], axis=1)

    qn, qr = w_uq[..., :NOPE_DIM], w_uq[..., NOPE_DIM:]
    qr_swap = jnp.concatenate([-qr[..., half:], qr[..., :half]], axis=-1)
    wq_cat = jnp.concatenate([qn, qr, qr_swap], axis=-1).reshape(d, Q_RANK, MLA_HEADS * HEAD_PAD).astype(BF)

    wk_pad = jnp.concatenate([w_uk, jnp.zeros_like(w_uk)], axis=-1).reshape(d, KV_RANK, MLA_HEADS * HEAD_PAD).astype(BF)
    wuvt = w_uv.reshape(d, KV_RANK, MLA_HEADS * V_DIM).transpose(0, 2, 1).astype(BF)
    wukt = jnp.concatenate([w_uk.transpose(0, 2, 3, 1),
                            jnp.zeros((d, MLA_HEADS, HEAD_PAD - NOPE_DIM, KV_RANK), F32)], axis=2).astype(BF)
    eye_h = jnp.eye(MLA_HEADS, dtype=F32)
    wuv_wide = jnp.einsum('lrhd,hg->lhrgd', w_uv, eye_h).reshape(d, MLA_HEADS, KV_RANK, MLA_HEADS * V_DIM).astype(BF)

    wa2_pad = jnp.zeros((d, LANE, GLA_KW), F32).at[:, 32:32 + GLA_GATE_RANK].set(w_a2).astype(BF)
    tri = jnp.tril(jnp.ones((GMLP_CHUNK, GMLP_CHUNK), bool))
    ws = jnp.where(tri[None, None], w_s, 0.0).astype(BF).reshape(d, GMLP_GROUPS // 2, 2 * GMLP_CHUNK, GMLP_CHUNK)
    gd = GMLP_WIDTH // GMLP_GROUPS
    bias_tab = jnp.repeat(b_s.transpose(0, 2, 1), gd, axis=2)
    ws00 = jnp.repeat(w_s[:, :, 0, 0], gd, axis=1)[:, None, :]
    bs0 = jnp.repeat(b_s[:, :, 0], gd, axis=1)[:, None, :]

    nff = D_FF // FF_CHUNK
    wg = w_gu[:, :, :D_FF].reshape(d, D_MODEL, nff, FF_CHUNK).transpose(0, 2, 1, 3).astype(BF)
    wu = w_gu[:, :, D_FF:].reshape(d, D_MODEL, nff, FF_CHUNK).transpose(0, 2, 1, 3).astype(BF)
    wd = w_down.reshape(d, nff, FF_CHUNK, D_MODEL).astype(BF)
    row = lambda a: a.reshape(d, 1, -1).astype(F32)
    return dict(
        wt=wt, small_t=small_t, wg=wg, wu=wu, wq_cat=wq_cat, wk_pad=wk_pad, wuvt=wuvt, wukt=wukt,
        wuv_wide=wuv_wide, wa2_pad=wa2_pad, ws=ws, bias_tab=bias_tab, ws00=ws00, bs0=bs0,
        g_mix=row(g_mix), g_q=row(g_q), g_kv=row(g_kv), g_v=row(g_v), b_v=row(b_v),
        b_a=row(b_a), g_gla=row(g_gla), g_ffn=row(g_ffn),
        wpa=w_pa.astype(BF), wpb=w_pb.astype(BF), wpc=w_pc.astype(BF), wo=w_o.astype(BF),
        wd=wd)


def _rope_tables(pos):
    half = ROPE_DIM // 2
    inv = jnp.power(ROPE_THETA, -jnp.arange(half, dtype=F32) * 2.0 / ROPE_DIM)
    ang = pos[:, None] * inv[None, :]
    cos, sin = jnp.cos(ang), jnp.sin(ang)
    t = pos.shape[0]
    ctab = jnp.concatenate([jnp.ones((t, NOPE_DIM), F32), cos, cos, jnp.zeros((t, 32), F32)], axis=1)
    stab = jnp.concatenate([jnp.zeros((t, NOPE_DIM), F32), sin, sin, jnp.zeros((t, 32), F32)], axis=1)
    return ctab, stab


def _front_kernel(prompt, tm, x_ref, ct_ref, st_ref, wt_ref, small_ref, wq_cat_ref, wkv_a_ref, wkv_b_ref,
                  wa2_ref, gm_a_ref, gm_b_ref, gmix_ref, gq_ref, gkv_ref, gv_ref, bv_ref, ba_ref, *outs):
    if prompt:
        (q_ref, k_ref, v_ref, ckv_ref, kr_ref, hb_ref, oq_ref, ok_ref, ov_ref, lf_ref, sg_ref) = outs
    else:
        (q_ref, ql_ref, ckv_ref, kr_ref, hb_ref, vn_ref, oq_ref, ok_ref, ov_ref, lf_ref, sg_ref) = outs
    x = x_ref[...]
    h = _rms(x, gmix_ref[...]).astype(BF)
    proj = lambda i: _dot_nt(h, wt_ref[_IN_OFF[i]:_IN_OFF[i + 1], :])
    ct = ct_ref[...]
    st = st_ref[...]
    scale = (NOPE_DIM + ROPE_DIM) ** -0.5 * math.log2(math.e)

    cq = _rms(proj(0), gq_ref[...]).astype(BF)
    qa = _dot(cq, wq_cat_ref[...])
    for hd in range(MLA_HEADS):
        sl = slice(hd * HEAD_PAD, (hd + 1) * HEAD_PAD)
        qh = (qa[:, sl] * ct + pltpu.roll(qa[:, sl], HEAD_PAD - ROPE_DIM, 1) * st) * scale
        q_ref[:, sl] = qh.astype(q_ref.dtype)
        if not prompt:
            ql_ref[:, hd * KV_RANK:(hd + 1) * KV_RANK] = _dot(qh.astype(BF), wkv_a_ref[hd])

    ckv = _rms(proj(1), gkv_ref[...])
    ckv_ref[...] = ckv
    small = _dot_nt(h, small_ref[...])
    lane = lax.broadcasted_iota(jnp.int32, small.shape, 1)
    kr_rot = jnp.where((lane >= NOPE_DIM) & (lane < NOPE_DIM + ROPE_DIM),
                       small * ct + pltpu.roll(small, NOPE_DIM, 1) * st, 0.0)
    kr_ref[...] = pltpu.roll(kr_rot, NOPE_DIM, 1)[:, :ROPE_DIM]
    if prompt:
        ckvb = ckv.astype(BF)
        kn = _dot(ckvb, wkv_a_ref[...])
        for hd in range(MLA_HEADS):
            sl = slice(hd * HEAD_PAD, (hd + 1) * HEAD_PAD)
            k_ref[:, sl] = (kn[:, sl] + kr_rot).astype(BF)
        vt = _dot_nt(wkv_b_ref[...], ckvb).astype(BF)
        ones = jnp.ones((V_DIM, ATTN_TILE), BF)
        for t in range(tm // ATTN_TILE):
            ts = slice(t * ATTN_TILE, (t + 1) * ATTN_TILE)
            for hd in range(MLA_HEADS):
                val = hd * HEAD_PAD + (hd % 2) * V_DIM
                pad = hd * HEAD_PAD + (1 - hd % 2) * V_DIM
                v_ref[t, val:val + V_DIM, :] = vt[hd * V_DIM:(hd + 1) * V_DIM, ts]
                v_ref[t, pad:pad + V_DIM, :] = ones

    u = proj(3)
    v = proj(4)
    mu = jnp.mean(v, axis=-1, keepdims=True)
    vc = v - mu
    var = jnp.mean(vc * vc, axis=-1, keepdims=True)
    vn = vc * lax.rsqrt(var + 1e-5) * gv_ref[...] + bv_ref[...]
    if prompt:
        vnb = vn.astype(BF)
        lane2 = lax.broadcasted_iota(jnp.int32, (GMLP_CHUNK, LANE), 1)
        gd = GMLP_WIDTH // GMLP_GROUPS
        for c in range(0, tm // GMLP_CHUNK, 2):
            ra = slice(c * GMLP_CHUNK, (c + 1) * GMLP_CHUNK)
            rb = slice((c + 1) * GMLP_CHUNK, (c + 2) * GMLP_CHUNK)
            for gp in range(GMLP_GROUPS // 2):
                cs = slice(gp * LANE, (gp + 1) * LANE)
                mm = _dot(gm_a_ref[gp], jnp.concatenate([vnb[ra, cs], vnb[rb, cs]], axis=1))
                for rs, ls in ((ra, slice(0, LANE)), (rb, slice(LANE, 2 * LANE))):
                    mixed = jnp.where(lane2 < gd, mm[:GMLP_CHUNK, ls], mm[GMLP_CHUNK:, ls])
                    hb_ref[rs, cs] = (u[rs, cs] * (mixed + gm_b_ref[:, cs])).astype(hb_ref.dtype)
    else:
        vn_ref[...] = vn
        hb_ref[...] = (u * (gm_a_ref[...] * vn + gm_b_ref[...])).astype(hb_ref.dtype)

    oq_ref[...] = proj(5) * (GLA_DK ** -0.5)
    ok_ref[...] = proj(6)
    ov_ref[...] = proj(7).astype(ov_ref.dtype)
    a = _dot(small.astype(BF), wa2_ref[...]) + ba_ref[...]
    lf_ref[...] = (jnp.minimum(a, 0.0) - jnp.log(1.0 + jnp.exp(-jnp.abs(a)))) * (1.0 / GLA_TAU)
    gg = proj(9)
    sg_ref[...] = (gg * jax.nn.sigmoid(gg)).astype(sg_ref.dtype)


def _front(x, ctab, stab, w, l, prompt, seq):
    r = x.shape[0]
    tm = ROW_TILE if prompt else r
    nt = r // tm
    row = lambda n: pl.BlockSpec((tm, n), lambda i: (i, 0))
    rows = lambda n, dt: (row(n), jax.ShapeDtypeStruct((r, n), dt))
    width = MLA_HEADS * HEAD_PAD
    if prompt:
        tab = pl.BlockSpec((tm, LANE), lambda i: (i % (seq // tm), 0))
        wkv_a, wkv_b = w['wk_pad'], w['wuvt']
        gm_a, gm_b = w['ws'], w['bias_tab']
        vt_out = (pl.BlockSpec((tm // ATTN_TILE, width, ATTN_TILE), lambda i: (i, 0, 0)),
                  jax.ShapeDtypeStruct((r // ATTN_TILE, width, ATTN_TILE), BF))
        outs = (rows(width, BF), rows(width, BF), vt_out, rows(KV_RANK, F32), rows(ROPE_DIM, F32),
                rows(GMLP_WIDTH, BF), rows(GLA_KW, F32), rows(GLA_KW, F32), rows(GLA_VW, BF), rows(GLA_KW, F32),
                rows(GLA_VW, BF))
    else:
        tab = pl.BlockSpec((tm, LANE), lambda i: (0, 0))
        wkv_a, wkv_b = w['wukt'], w['wuvt']
        gm_a, gm_b = w['ws00'], w['bs0']
        outs = (rows(width, F32), rows(MLA_HEADS * KV_RANK, F32), rows(KV_RANK, F32), rows(ROPE_DIM, F32),
                rows(GMLP_WIDTH, BF), rows(GMLP_WIDTH, F32), rows(GLA_KW, F32), rows(GLA_KW, F32), rows(GLA_VW, F32),
                rows(GLA_KW, F32), rows(GLA_VW, F32))
    consts = (w['wt'], w['small_t'], w['wq_cat'], wkv_a, wkv_b, w['wa2_pad'], gm_a, gm_b,
              w['g_mix'], w['g_q'], w['g_kv'], w['g_v'], w['b_v'], w['b_a'])
    return pl.pallas_call(
        functools.partial(_front_kernel, prompt, tm),
        grid=(nt,),
        in_specs=[row(D_MODEL), tab, tab] + [_layer_spec(c, l) for c in consts],
        out_specs=[o[0] for o in outs],
        out_shape=[o[1] for o in outs],
        compiler_params=_params(("parallel",)),
        name="front_prompt" if prompt else "front_sample",
    )(x, ctab, stab, *consts)


def _attn_kernel(q_ref, k_ref, vt_ref, o_ref, m_ref, acc_ref, st_ref, p_ref):
    tk, tq = ATTN_TILE, ATTN_Q_TILE
    ratio = tq // tk
    qi = pl.program_id(1)
    m_ref[...] = jnp.full(m_ref.shape, -jnp.inf, F32)
    acc_ref[...] = jnp.zeros(acc_ref.shape, F32)
    krow = lax.broadcasted_iota(jnp.int32, (tk, tq), 0)
    qcol = lax.broadcasted_iota(jnp.int32, (tk, tq), 1)

    def tile(j, diag):
        off = pl.multiple_of(j * tk, tk)
        for h0 in range(0, MLA_HEADS, ATTN_HEAD_BATCH):
            heads = [(hd, slice(hd * HEAD_PAD, (hd + 1) * HEAD_PAD)) for hd in range(h0, h0 + ATTN_HEAD_BATCH)]
            mx = {}
            for hd, cs in heads:
                st = _dot_nt(k_ref[pl.ds(off, tk), cs], q_ref[:, cs])
                if diag is not None:
                    st = jnp.where(krow + diag * tk <= qcol, st, -jnp.inf)
                st_ref[hd] = st
                mx[hd] = jnp.max(st, axis=0, keepdims=True)
            alpha = {}
            for hd, cs in heads:
                m = m_ref[hd]
                m_new = jnp.maximum(m, mx[hd])
                alpha[hd] = jnp.exp2(m - m_new)
                m_ref[hd] = m_new
                p_ref[hd] = jnp.exp2(st_ref[hd] - m_new).astype(BF)
            for hd, cs in heads:
                r0 = (hd % 2) * (V_DIM - ATTN_ONES_ROWS)
                rs = slice(r0, r0 + V_DIM + ATTN_ONES_ROWS)
                vrows = slice(hd * HEAD_PAD + r0, hd * HEAD_PAD + r0 + V_DIM + ATTN_ONES_ROWS)
                acc_ref[hd, rs, :] = alpha[hd] * acc_ref[hd, rs, :] + _dot(vt_ref[j, vrows, :], p_ref[hd])

    def body(j, carry):
        tile(j, None)
        return carry

    lax.fori_loop(0, qi * ratio, body, 0)
    for d in range(ratio):
        tile(qi * ratio + d, d)
    rowi = lax.broadcasted_iota(jnp.int32, (HEAD_PAD, tq), 0)
    for pr in range(MLA_HEADS // 2):
        ae = acc_ref[2 * pr]
        ao = acc_ref[2 * pr + 1]
        out_t = jnp.where(rowi < V_DIM, ae / ae[V_DIM:V_DIM + 1, :], ao / ao[V_DIM - 1:V_DIM, :])
        o_ref[:, pr * HEAD_PAD:(pr + 1) * HEAD_PAD] = out_t.T.astype(o_ref.dtype)


def _attention(q, k, vt, batch, seq):
    tk, tq = ATTN_TILE, ATTN_Q_TILE
    width = MLA_HEADS * HEAD_PAD
    q3, k3 = (a.reshape(batch, seq, width) for a in (q, k))
    vt4 = vt.reshape(batch, seq // tk, width, tk)
    out = pl.pallas_call(
        _attn_kernel,
        grid=(batch, seq // tq),
        in_specs=[pl.BlockSpec((None, tq, width), lambda b, i: (b, i, 0)),
                  pl.BlockSpec((None, seq, width), lambda b, i: (b, 0, 0)),
                  pl.BlockSpec((None, seq // tk, width, tk), lambda b, i: (b, 0, 0, 0))],
        out_specs=pl.BlockSpec((None, tq, MLA_HEADS * V_DIM), lambda b, i: (b, i, 0)),
        out_shape=jax.ShapeDtypeStruct((batch, seq, MLA_HEADS * V_DIM), BF),
        scratch_shapes=[pltpu.VMEM((MLA_HEADS, 1, tq), F32), pltpu.VMEM((MLA_HEADS, HEAD_PAD, tq), F32),
                        pltpu.VMEM((MLA_HEADS, tk, tq), F32), pltpu.VMEM((MLA_HEADS, tk, tq), BF)],
        compiler_params=_params(("parallel", "arbitrary")),
        name="mla_prompt_attn",
    )(q3, k3, vt4)
    return out.reshape(batch * seq, MLA_HEADS * V_DIM)


_GLA_LEVELS = (1, 2, 4, 8, 16, 32, 64)


def _gla_tables():
    c = GLA_CHUNK
    t = np.arange(c)[:, None]
    r = np.arange(c)[None, :]
    blocks = []
    for m in _GLA_LEVELS[1:] + (c,):
        blocks.append(((t // m == r // m) & (r <= t)))
    for m in _GLA_LEVELS[1:] + (c,):
        blocks.append(((t // m == r // m) & (r > t)))
    lstack = np.concatenate(blocks, axis=0).astype(np.float32)
    lstack = np.concatenate([lstack, lstack], axis=1)
    masks = [np.eye(c, dtype=np.float32)]
    for m in _GLA_LEVELS:
        masks.append((((t // m) % 2 == 1) & (r // m == t // m - 1)).astype(np.float32))
    return jnp.asarray(lstack, BF), jnp.asarray(np.stack(masks), F32)


def _gla_kernel(nchunk, q_ref, k_ref, v_ref, lf_ref, sg_ref, l_ref, mask_ref, g_ref, hc_ref, s_ref,
                px_ref, qt_ref, kt_ref, z_ref):
    c = GLA_CHUNK
    ci = pl.program_id(1)
    nlev = len(_GLA_LEVELS)
    npair = GLA_HEADS // 2

    @pl.when(ci == 0)
    def _():
        z_ref[...] = jnp.zeros_like(z_ref)

    lf = lf_ref[...]
    hi = lf.astype(BF)
    lo = (lf - hi.astype(F32)).astype(BF)
    px_ref[...] = _dot(l_ref[...], jnp.concatenate([hi, lo], axis=0))
    q = q_ref[...]
    k = k_ref[...]
    lane = lax.broadcasted_iota(jnp.int32, (c, LANE), 1)
    low = lane < GLA_DK

    def put(idx, qv, kv):
        for p in range(npair):
            qp = qv[:, p * LANE:(p + 1) * LANE]
            qt_ref[idx, p, 0:c, :] = jnp.where(low, qp, 0.0).astype(BF)
            qt_ref[idx, p, c:2 * c, :] = jnp.where(low, 0.0, qp).astype(BF)
        kt_ref[idx] = kv.astype(BF)

    pblk = lambda i: px_ref[i * c:(i + 1) * c, :]
    put(0, q, k)
    put(1, q * jnp.exp(lf), k)
    for i in range(nlev - 1):
        put(2 + i, q * jnp.exp(pblk(i)), k * jnp.exp(pblk(nlev + i)))
    bcum = pblk(nlev - 1)
    put(nlev + 1, q * jnp.exp(bcum), k * jnp.exp(pblk(2 * nlev - 1)))
    decay = jnp.exp(bcum[c - 1:c, :])

    for p in range(npair):
        a = jnp.zeros((2 * c, c), F32)
        for lv in range(nlev + 1):
            mk = mask_ref[lv]
            sc = _dot_nt(qt_ref[lv, p], kt_ref[lv, :, p * LANE:(p + 1) * LANE])
            a = a + sc * jnp.concatenate([mk, mk], axis=0)
        kx = kt_ref[nlev + 1, :, p * LANE:(p + 1) * LANE]
        for hh in range(2):
            hd = 2 * p + hh
            vs = slice(hd * GLA_DV, (hd + 1) * GLA_DV)
            vh = v_ref[:, vs]
            z = z_ref[hd]
            o = (_dot(a[hh * c:(hh + 1) * c].astype(BF), vh)
                 + _dot_nt(qt_ref[nlev + 1, p, hh * c:(hh + 1) * c, :], z.astype(BF)))
            y = _rms(o, g_ref[...])
            hc_ref[:, vs] = (y * sg_ref[:, vs].astype(F32)).astype(hc_ref.dtype)
            z_ref[hd] = z * decay[:, p * LANE:(p + 1) * LANE] + _dot_tn(vh, kx)

    @pl.when(ci == nchunk - 1)
    def _():
        for hd in range(GLA_HEADS):
            zt = z_ref[hd].T
            s_ref[hd] = zt[(hd % 2) * GLA_DK:(hd % 2 + 1) * GLA_DK, :]


def _gla(gq, gk, gv, lf, sg, g_gla, l, batch, seq):
    c = GLA_CHUNK
    nchunk = seq // c
    lstack, masks = _gla_tables()
    nlev = len(_GLA_LEVELS)
    r3 = lambda a: a.reshape(batch, seq, a.shape[-1])
    blk = lambda n: pl.BlockSpec((None, c, n), lambda b, i: (b, i, 0))
    hc, s = pl.pallas_call(
        functools.partial(_gla_kernel, nchunk),
        grid=(batch, nchunk),
        in_specs=[blk(GLA_KW), blk(GLA_KW), blk(GLA_VW), blk(GLA_KW), blk(GLA_VW),
                  _const_spec(lstack.shape), _const_spec(masks.shape), _layer_spec(g_gla, l)],
        out_specs=[blk(GLA_VW), pl.BlockSpec((None, GLA_HEADS, GLA_DK, GLA_DV), lambda b, i: (b, 0, 0, 0))],
        out_shape=[jax.ShapeDtypeStruct((batch, seq, GLA_VW), BF),
                   jax.ShapeDtypeStruct((batch, GLA_HEADS, GLA_DK, GLA_DV), F32)],
        scratch_shapes=[pltpu.VMEM((2 * nlev * c, GLA_KW), F32),
                        pltpu.VMEM((nlev + 2, GLA_HEADS // 2, 2 * c, LANE), BF),
                        pltpu.VMEM((nlev + 2, c, GLA_KW), BF),
                        pltpu.VMEM((GLA_HEADS, GLA_DV, LANE), F32)],
        compiler_params=_params(("parallel", "arbitrary")),
        name="gla_prompt",
    )(r3(gq), r3(gk), r3(gv), r3(lf), r3(sg), lstack, masks, g_gla)
    return hc.reshape(batch * seq, GLA_VW), s


def _merge_kernel(x_ref, ha_ref, hb_ref, hc_ref, gmix_ref, wt_ref, wpa_ref, wpb_ref, wpc_ref, wo_ref, o_ref):
    x = x_ref[...]
    h = _rms(x, gmix_ref[...]).astype(BF)
    mix = None
    for j, (hr, wr) in enumerate(((ha_ref, wpa_ref), (hb_ref, wpb_ref), (hc_ref, wpc_ref))):
        g0 = _IN_OFF[10] + j * D_MODEL
        gate = jax.nn.sigmoid(_dot_nt(h, wt_ref[g0:g0 + D_MODEL, :]))
        term = gate * _dot(hr[...].astype(BF), wr[...])
        mix = term if mix is None else mix + term
    o_ref[...] = x + _dot(mix.astype(BF), wo_ref[...])


def _merge(x, ha, hb, hc, w, l, tm):
    r = x.shape[0]
    row = lambda n: pl.BlockSpec((tm, n), lambda i: (i, 0))
    consts = (w['g_mix'], w['wt'], w['wpa'], w['wpb'], w['wpc'], w['wo'])
    return pl.pallas_call(
        _merge_kernel,
        grid=(r // tm,),
        in_specs=[row(D_MODEL), row(ha.shape[1]), row(hb.shape[1]), row(hc.shape[1])]
                 + [_layer_spec(c, l) for c in consts],
        out_specs=row(D_MODEL),
        out_shape=jax.ShapeDtypeStruct((r, D_MODEL), F32),
        compiler_params=_params(("parallel",)),
        name="merge",
    )(x, ha, hb, hc, *consts)


def _ffn_kernel(final, x_ref, g_ref, wg_ref, wu_ref, wd_ref, gf_ref, o_ref):
    x = x_ref[...]
    hn = _rms(x, g_ref[...]).astype(BF)
    acc = jnp.zeros(x.shape, F32)
    for cidx in range(D_FF // FF_CHUNK):
        g = _dot(hn, wg_ref[cidx])
        u = _dot(hn, wu_ref[cidx])
        act = (g * jax.nn.sigmoid(g) * u).astype(BF)
        acc = acc + _dot(act, wd_ref[cidx])
    y = x + acc
    if final:
        y = _rms(y, gf_ref[...])
    o_ref[...] = y


def _ffn(x, w, l, g_final, final, tm):
    r = x.shape[0]
    row = pl.BlockSpec((tm, D_MODEL), lambda i: (i, 0))
    consts = (w['g_ffn'], w['wg'], w['wu'], w['wd'], g_final)
    return pl.pallas_call(
        functools.partial(_ffn_kernel, final),
        grid=(r // tm,),
        in_specs=[row] + [_layer_spec(c, l) for c in consts[:-1]] + [_const_spec(g_final.shape)],
        out_specs=row,
        out_shape=jax.ShapeDtypeStruct((r, D_MODEL), F32),
        compiler_params=_params(("parallel",)),
        name="ffn",
    )(x, *consts)


GLA_STEP_BATCH = 8
DECODE_PAGES = 32
PAGE_GROUP = 8


def _decode_kernel(layer, nch, pt_ref, ql_ref, qr_ref, cn_ref, kn_ref, ckv_hbm, krt_hbm, o_ref,
                   cbuf, rbuf, cb, rb, sem, m_ref, l_ref, acc_ref):
    npg = DECODE_PAGES
    total = ql_ref.shape[0] * nch

    def page_copies(slot, i, pid):
        rows = pl.ds(i * PAGE_SIZE, PAGE_SIZE)
        return (pltpu.make_async_copy(ckv_hbm.at[layer, pid], cbuf.at[slot, rows, :], sem.at[0, slot]),
                pltpu.make_async_copy(krt_hbm.at[layer, pid], rbuf.at[slot, :, rows], sem.at[1, slot]))

    def start_pages(step, slot, lo, hi):
        b = step // nch
        c = step % nch
        for i in range(lo, hi):
            for cp in page_copies(slot, i, pt_ref[b, c * npg + i]):
                cp.start()

    def wait(slot):
        for i in range(npg):
            for cp in page_copies(slot, i, 0):
                cp.wait()

    start_pages(0, 0, 0, npg)
    ngroup = npg // PAGE_GROUP
    span = PAGE_GROUP * PAGE_SIZE
    half = KV_RANK // 2

    def body(step, carry):
        slot = step % 2
        b = step // nch
        c = step % nch
        nxt = jnp.minimum(step + 1, total - 1)
        ql = ql_ref[b]
        qr = qr_ref[b]

        @pl.when(c == 0)
        def _():
            cn = cn_ref[b]
            m_ref[...] = (jnp.sum(ql * cn, axis=-1, keepdims=True) + jnp.sum(qr * kn_ref[b], axis=-1, keepdims=True))
            l_ref[...] = jnp.ones_like(l_ref)
            acc_ref[...] = jnp.broadcast_to(cn, acc_ref.shape)

        wait(slot)
        qlb = ql.astype(BF)
        qrb = qr.astype(BF)
        ss = []
        for g in range(ngroup):
            rows = slice(g * span, (g + 1) * span)
            cb[rows, :] = cbuf[slot, rows, :].astype(BF)
            rb[:, rows] = rbuf[slot, :, rows].astype(BF)
            ss.append(_dot_nt(qlb, cb[rows, :]) + _dot(qrb, rb[:, rows]))
            start_pages(nxt, 1 - slot, g * PAGE_GROUP, (g + 1) * PAGE_GROUP)
        parts = []
        for hf in range(2):
            gs = range(hf * ngroup // 2, (hf + 1) * ngroup // 2)
            mh = ss[gs[0]].max(axis=-1, keepdims=True)
            for g in gs[1:]:
                mh = jnp.maximum(mh, ss[g].max(axis=-1, keepdims=True))
            lh = jnp.zeros_like(mh)
            ah = jnp.zeros((MLA_HEADS, KV_RANK), F32)
            for g in gs:
                p = jnp.exp2(ss[g] - mh)
                lh = lh + jnp.sum(p, axis=-1, keepdims=True)
                pb = p.astype(BF)
                rows = slice(g * span, (g + 1) * span)
                ah = ah + jnp.concatenate([_dot(pb, cb[rows, :half]), _dot(pb, cb[rows, half:])], axis=1)
            parts.append((mh, lh, ah))
        m = m_ref[...]
        m_new = jnp.maximum(jnp.maximum(m, parts[0][0]), parts[1][0])
        alpha = jnp.exp2(m - m_new)
        l = alpha * l_ref[...]
        acc = alpha * acc_ref[...]
        for mh, lh, ah in parts:
            wh = jnp.exp2(mh - m_new)
            l = l + wh * lh
            acc = acc + wh * ah
        m_ref[...] = m_new
        l_ref[...] = l
        acc_ref[...] = acc

        @pl.when(c == nch - 1)
        def _():
            o_ref[b] = acc / l

        return carry

    lax.fori_loop(0, total, body, 0)
    wait(total % 2)


def _decode(layer, page_table, ql, qr, ckv_new, kr_new, cache_ckv, cache_krope_t):
    b, npages = page_table.shape
    npg = DECODE_PAGES
    assert npages % npg == 0 and npg % PAGE_GROUP == 0
    keys = npg * PAGE_SIZE
    vmem = pl.BlockSpec(memory_space=pltpu.VMEM)
    hbm = pl.BlockSpec(memory_space=pl.ANY)
    return pl.pallas_call(
        functools.partial(_decode_kernel, layer, npages // npg),
        in_specs=[pl.BlockSpec(memory_space=pltpu.SMEM), vmem, vmem, vmem, vmem, hbm, hbm],
        out_specs=vmem,
        out_shape=jax.ShapeDtypeStruct((b, MLA_HEADS, KV_RANK), F32),
        scratch_shapes=[pltpu.VMEM((2, keys, KV_RANK), F32), pltpu.VMEM((2, ROPE_DIM, keys), F32),
                        pltpu.VMEM((keys, KV_RANK), BF), pltpu.VMEM((ROPE_DIM, keys), BF),
                        pltpu.SemaphoreType.DMA((2, 2)),
                        pltpu.VMEM((MLA_HEADS, 1), F32), pltpu.VMEM((MLA_HEADS, 1), F32),
                        pltpu.VMEM((MLA_HEADS, KV_RANK), F32)],
        compiler_params=pltpu.CompilerParams(vmem_limit_bytes=VMEM_LIMIT),
        name="mla_decode",
    )(page_table, ql, qr, ckv_new, kr_new, cache_ckv, cache_krope_t)


def _sample_ha_kernel(ol_ref, w_ref, o_ref):
    acc = None
    for hd in range(MLA_HEADS):
        t = _dot(ol_ref[hd].astype(BF), w_ref[hd])
        acc = t if acc is None else acc + t
    o_ref[...] = acc


def _sample_ha(o_lat_t, wuv_wide, l):
    b = o_lat_t.shape[1]
    return pl.pallas_call(
        _sample_ha_kernel,
        grid=(1,),
        in_specs=[_const_spec(o_lat_t.shape), _layer_spec(wuv_wide, l)],
        out_specs=pl.BlockSpec((b, MLA_HEADS * V_DIM), lambda i: (0, 0)),
        out_shape=jax.ShapeDtypeStruct((b, MLA_HEADS * V_DIM), F32),
        compiler_params=_params(("arbitrary",)),
        name="sample_ha",
    )(o_lat_t, wuv_wide)


def _gla_step_kernel(q_ref, k_ref, v_ref, lf_ref, sg_ref, g_ref, s0_ref, hc_ref, s_ref):
    dk = GLA_DK
    eye = lax.broadcasted_iota(jnp.int32, (dk, dk), 0) == lax.broadcasted_iota(jnp.int32, (dk, dk), 1)
    col = lambda rowv: jnp.sum(jnp.where(eye, jnp.broadcast_to(rowv, (dk, dk)), 0.0), axis=-1, keepdims=True)
    for bi in range(q_ref.shape[0]):
        for hd in range(GLA_HEADS):
            ks = slice(hd * dk, (hd + 1) * dk)
            vs = slice(hd * GLA_DV, (hd + 1) * GLA_DV)
            s_new = jnp.exp(col(lf_ref[bi, :, ks])) * s0_ref[bi, hd] + col(k_ref[bi, :, ks]) * v_ref[bi, :, vs]
            s_ref[bi, hd] = s_new
            o = jnp.sum(col(q_ref[bi, :, ks]) * s_new, axis=0, keepdims=True)
            hc_ref[bi, :, vs] = _rms(o, g_ref[...]) * sg_ref[bi, :, vs]


def _gla_step(gq, gk, gv, lf, sg, g_gla, state_all, l):
    b = gq.shape[0]
    gb = GLA_STEP_BATCH
    assert b % gb == 0
    r3 = lambda a: a.reshape(b, 1, a.shape[-1])
    blk = lambda n: pl.BlockSpec((gb, 1, n), lambda i: (i, 0, 0))
    sshape = (GLA_HEADS, GLA_DK, GLA_DV)
    hc, s = pl.pallas_call(
        _gla_step_kernel,
        grid=(b // gb,),
        in_specs=[blk(GLA_KW), blk(GLA_KW), blk(GLA_VW), blk(GLA_KW), blk(GLA_VW), _layer_spec(g_gla, l),
                  pl.BlockSpec((None, gb) + sshape, lambda i: (l, i, 0, 0, 0))],
        out_specs=[blk(GLA_VW), pl.BlockSpec((gb,) + sshape, lambda i: (i, 0, 0, 0))],
        out_shape=[jax.ShapeDtypeStruct((b, 1, GLA_VW), F32), jax.ShapeDtypeStruct((b,) + sshape, F32)],
        compiler_params=_params(("parallel",)),
        name="gla_step",
    )(r3(gq), r3(gk), r3(gv), r3(lf), r3(sg), g_gla, state_all)
    return hc.reshape(b, GLA_VW), s


def kernel(x_prompt, x_sample, cache_ckv, cache_krope, state_gla, page_table, g_mix, w_in, g_q, w_uq, g_kv, w_uk, w_uv, g_v, b_v, w_s, b_s, w_a2, b_a, g_gla, w_pa, w_pb, w_pc, w_o, g_ffn, w_gu, w_down, g_final):
    batch, seq, _ = x_prompt.shape
    dec_b, dec_t, _ = x_sample.shape
    assert dec_t == 1 and seq % ROW_TILE == 0
    depth = w_in.shape[0]
    past = page_table.shape[1] * PAGE_SIZE
    ct_p, st_p = _rope_tables(jnp.arange(seq, dtype=F32))
    ct_s, st_s = _rope_tables(jnp.full((dec_b,), float(past), F32))
    gfin = g_final.reshape(1, -1).astype(F32)
    cache_krope_t = jnp.swapaxes(cache_krope, 2, 3)

    xp = x_prompt.reshape(batch * seq, D_MODEL)
    xs = x_sample.reshape(dec_b, D_MODEL)
    w = _prep_weights(g_mix, w_in, g_q, w_uq, g_kv, w_uk, w_uv, g_v, b_v, w_s, b_s, w_a2, b_a, g_gla,
                      w_pa, w_pb, w_pc, w_o, g_ffn, w_gu, w_down)
    ckv_p, kr_p, gla_p, ckv_s, kr_s, gla_s, gv_s = [], [], [], [], [], [], []
    for l in range(depth):
        last = l == depth - 1
        q, k, v, ckv, kr, hb, gq, gk, gv, lf, sg = _front(xp, ct_p, st_p, w, l, True, seq)
        ha = _attention(q, k, v, batch, seq)
        hc, s_c = _gla(gq, gk, gv, lf, sg, w['g_gla'], l, batch, seq)
        xp = _ffn(_merge(xp, ha, hb, hc, w, l, ROW_TILE), w, l, gfin, last, ROW_TILE)
        ckv_p.append(ckv.reshape(batch, seq, KV_RANK))
        kr_p.append(kr.reshape(batch, seq, ROPE_DIM))
        gla_p.append(s_c)
        q, ql, ckv, kr, hb, vn, gq, gk, gv, lf, sg = _front(xs, ct_s, st_s, w, l, False, seq)
        qr = q.reshape(dec_b, MLA_HEADS, HEAD_PAD)[:, :, NOPE_DIM:NOPE_DIM + ROPE_DIM]
        o_lat = _decode(l, page_table, ql.reshape(dec_b, MLA_HEADS, KV_RANK), qr,
                        ckv.reshape(dec_b, 1, KV_RANK), kr.reshape(dec_b, 1, ROPE_DIM), cache_ckv, cache_krope_t)
        ha = _sample_ha(o_lat.transpose(1, 0, 2), w['wuv_wide'], l)
        hc, s_c = _gla_step(gq, gk, gv, lf, sg, w['g_gla'], state_gla, l)
        xs = _ffn(_merge(xs, ha, hb, hc, w, l, dec_b), w, l, gfin, last, dec_b)
        ckv_s.append(ckv.reshape(dec_b, 1, KV_RANK))
        kr_s.append(kr.reshape(dec_b, 1, ROPE_DIM))
        gla_s.append(s_c)
        gv_s.append(vn.reshape(dec_b, 1, GMLP_WIDTH))
    return (xp.reshape(batch, seq, D_MODEL), xs.reshape(dec_b, 1, D_MODEL), jnp.stack(ckv_p), jnp.stack(kr_p),
            jnp.stack(gla_p), jnp.stack(ckv_s), jnp.stack(kr_s), jnp.stack(gla_s), jnp.stack(gv_s))
```

```python
import functools
import math

import numpy as np
import jax
import jax.numpy as jnp
from jax import lax
from jax.experimental import pallas as pl
from jax.experimental.pallas import tpu as pltpu

D_MODEL = 1024
PAGE_SIZE = 128
MLA_HEADS = 8
Q_RANK = 384
KV_RANK = 256
NOPE_DIM = 64
ROPE_DIM = 32
V_DIM = 64
ROPE_THETA = 10000.0
GMLP_GROUPS = 8
GMLP_CHUNK = 128
GMLP_WIDTH = 512
GLA_HEADS = 4
GLA_DK = 64
GLA_DV = 128
GLA_KW = GLA_HEADS * GLA_DK
GLA_VW = GLA_HEADS * GLA_DV
GLA_GATE_RANK = 16
GLA_TAU = 16.0
N_BRANCH = 3
D_FF = 2816
IN_SPLITS = (Q_RANK, KV_RANK, ROPE_DIM, GMLP_WIDTH, GMLP_WIDTH, GLA_KW, GLA_KW, GLA_VW,
             GLA_GATE_RANK, GLA_VW, N_BRANCH * D_MODEL)
_IN_OFF = tuple(int(v) for v in np.cumsum((0,) + IN_SPLITS))

LANE = 128
SUB = 8
HEAD_PAD = 128
GLA_CHUNK = 128
ROW_TILE = 512
ATTN_TILE = 256
ATTN_Q_TILE = 512
ATTN_HEAD_BATCH = 8
ATTN_ONES_ROWS = 16
FF_CHUNK = 256
VMEM_LIMIT = 56 * 1024 * 1024

BF = jnp.bfloat16
F32 = jnp.float32


def _dot(a, b):
    return jnp.dot(a, b, preferred_element_type=F32)


def _dot_nt(a, b):
    return lax.dot_general(a, b, (((1,), (1,)), ((), ())), preferred_element_type=F32)


def _dot_tn(a, b):
    return lax.dot_general(a, b, (((0,), (0,)), ((), ())), preferred_element_type=F32)


def _rms(x, g, eps=1e-6):
    return x * lax.rsqrt(jnp.mean(x * x, axis=-1, keepdims=True) + eps) * g


def _const_spec(shape):
    nd = len(shape)
    return pl.BlockSpec(shape, lambda *_: (0,) * nd, pipeline_mode=pl.Buffered(1))


def _layer_spec(arr, l):
    nd = arr.ndim - 1
    return pl.BlockSpec((None,) + arr.shape[1:], lambda *_: (l,) + (0,) * nd, pipeline_mode=pl.Buffered(1))


def _params(sem):
    return pltpu.CompilerParams(dimension_semantics=sem, vmem_limit_bytes=VMEM_LIMIT)


def _prep_weights(g_mix, w_in, g_q, w_uq, g_kv, w_uk, w_uv, g_v, b_v, w_s, b_s, w_a2, b_a, g_gla,
                  w_pa, w_pb, w_pc, w_o, g_ffn, w_gu, w_down):
    d = w_in.shape[0]
    half = ROPE_DIM // 2
    wt = jnp.swapaxes(w_in, 1, 2).astype(BF)
    kr = wt[:, _IN_OFF[2]:_IN_OFF[3]]
    kr_swap = jnp.concatenate([-kr[:, half:], kr[:, :half]], axis=1)
    z16 = jnp.zeros((d, 16, D_MODEL), BF)
    z32 = jnp.zeros((d, 32, D_MODEL), BF)
    small_t = jnp.concatenate([kr_swap, wt[:, _IN_OFF[8]:_IN_OFF[9]], z16, kr, z32], axis=1)

    qn, qr = w_uq[..., :NOPE_DIM], w_uq[..., NOPE_DIM:]
    qr_swap = jnp.concatenate([-qr[..., half:], qr[..., :half]], axis=-1)
    wq_cat = jnp.concatenate([qn, qr, qr_swap], axis=-1).reshape(d, Q_RANK, MLA_HEADS * HEAD_PAD).astype(BF)

    wk_pad = jnp.concatenate([w_uk, jnp.zeros_like(w_uk)], axis=-1).reshape(d, KV_RANK, MLA_HEADS * HEAD_PAD).astype(BF)
    wuvt = w_uv.reshape(d, KV_RANK, MLA_HEADS * V_DIM).transpose(0, 2, 1).astype(BF)
    wukt = jnp.concatenate([w_uk.transpose(0, 2, 3, 1),
                            jnp.zeros((d, MLA_HEADS, HEAD_PAD - NOPE_DIM, KV_RANK), F32)], axis=2).astype(BF)
    eye_h = jnp.eye(MLA_HEADS, dtype=F32)
    wuv_wide = jnp.einsum('lrhd,hg->lhrgd', w_uv, eye_h).reshape(d, MLA_HEADS, KV_RANK, MLA_HEADS * V_DIM).astype(BF)

    wa2_pad = jnp.zeros((d, LANE, GLA_KW), F32).at[:, 32:32 + GLA_GATE_RANK].set(w_a2).astype(BF)
    tri = jnp.tril(jnp.ones((GMLP_CHUNK, GMLP_CHUNK), bool))
    ws = jnp.where(tri[None, None], w_s, 0.0).astype(BF).reshape(d, GMLP_GROUPS // 2, 2 * GMLP_CHUNK, GMLP_CHUNK)
    gd = GMLP_WIDTH // GMLP_GROUPS
    bias_tab = jnp.repeat(b_s.transpose(0, 2, 1), gd, axis=2)
    ws00 = jnp.repeat(w_s[:, :, 0, 0], gd, axis=1)[:, None, :]
    bs0 = jnp.repeat(b_s[:, :, 0], gd, axis=1)[:, None, :]

    nff = D_FF // FF_CHUNK
    wgu = jnp.swapaxes(jnp.swapaxes(w_gu, 1, 2).reshape(d, 2, nff, FF_CHUNK, D_MODEL), 3, 4).astype(BF)
    wd = w_down.reshape(d, nff, FF_CHUNK, D_MODEL).astype(BF)
    row = lambda a: a.reshape(d, 1, -1).astype(F32)
    return dict(
        wt=wt, small_t=small_t, wgu=wgu, wq_cat=wq_cat, wk_pad=wk_pad, wuvt=wuvt, wukt=wukt,
        wuv_wide=wuv_wide, wa2_pad=wa2_pad, ws=ws, bias_tab=bias_tab, ws00=ws00, bs0=bs0,
        g_mix=row(g_mix), g_q=row(g_q), g_kv=row(g_kv), g_v=row(g_v), b_v=row(b_v),
        b_a=row(b_a), g_gla=row(g_gla), g_ffn=row(g_ffn),
        wpa=w_pa.astype(BF), wpb=w_pb.astype(BF), wpc=w_pc.astype(BF), wo=w_o.astype(BF),
        wd=wd)


def _rope_tables(pos):
    half = ROPE_DIM // 2
    inv = jnp.power(ROPE_THETA, -jnp.arange(half, dtype=F32) * 2.0 / ROPE_DIM)
    ang = pos[:, None] * inv[None, :]
    cos, sin = jnp.cos(ang), jnp.sin(ang)
    t = pos.shape[0]
    ctab = jnp.concatenate([jnp.ones((t, NOPE_DIM), F32), cos, cos, jnp.zeros((t, 32), F32)], axis=1)
    stab = jnp.concatenate([jnp.zeros((t, NOPE_DIM), F32), sin, sin, jnp.zeros((t, 32), F32)], axis=1)
    return ctab, stab


def _front_kernel(prompt, tm, x_ref, ct_ref, st_ref, wt_ref, small_ref, wq_cat_ref, wkv_a_ref, wkv_b_ref,
                  wa2_ref, gm_a_ref, gm_b_ref, gmix_ref, gq_ref, gkv_ref, gv_ref, bv_ref, ba_ref, *outs):
    if prompt:
        (q_ref, k_ref, v_ref, ckv_ref, kr_ref, hb_ref, oq_ref, ok_ref, ov_ref, lf_ref, sg_ref) = outs
    else:
        (q_ref, ql_ref, ckv_ref, kr_ref, hb_ref, vn_ref, oq_ref, ok_ref, ov_ref, lf_ref, sg_ref) = outs
    x = x_ref[...]
    h = _rms(x, gmix_ref[...]).astype(BF)
    proj = lambda i: _dot_nt(h, wt_ref[_IN_OFF[i]:_IN_OFF[i + 1], :])
    ct = ct_ref[...]
    st = st_ref[...]
    scale = (NOPE_DIM + ROPE_DIM) ** -0.5 * math.log2(math.e)

    cq = _rms(proj(0), gq_ref[...]).astype(BF)
    qa = _dot(cq, wq_cat_ref[...])
    for hd in range(MLA_HEADS):
        sl = slice(hd * HEAD_PAD, (hd + 1) * HEAD_PAD)
        qh = (qa[:, sl] * ct + pltpu.roll(qa[:, sl], HEAD_PAD - ROPE_DIM, 1) * st) * scale
        q_ref[:, sl] = qh.astype(q_ref.dtype)
        if not prompt:
            ql_ref[:, hd * KV_RANK:(hd + 1) * KV_RANK] = _dot(qh.astype(BF), wkv_a_ref[hd])

    ckv = _rms(proj(1), gkv_ref[...])
    ckv_ref[...] = ckv
    small = _dot_nt(h, small_ref[...])
    lane = lax.broadcasted_iota(jnp.int32, small.shape, 1)
    kr_rot = jnp.where((lane >= NOPE_DIM) & (lane < NOPE_DIM + ROPE_DIM),
                       small * ct + pltpu.roll(small, NOPE_DIM, 1) * st, 0.0)
    kr_ref[...] = pltpu.roll(kr_rot, NOPE_DIM, 1)[:, :ROPE_DIM]
    if prompt:
        ckvb = ckv.astype(BF)
        kn = _dot(ckvb, wkv_a_ref[...])
        for hd in range(MLA_HEADS):
            sl = slice(hd * HEAD_PAD, (hd + 1) * HEAD_PAD)
            k_ref[:, sl] = (kn[:, sl] + kr_rot).astype(BF)
        vt = _dot_nt(wkv_b_ref[...], ckvb).astype(BF)
        ones = jnp.ones((V_DIM, ATTN_TILE), BF)
        for t in range(tm // ATTN_TILE):
            ts = slice(t * ATTN_TILE, (t + 1) * ATTN_TILE)
            for hd in range(MLA_HEADS):
                val = hd * HEAD_PAD + (hd % 2) * V_DIM
                pad = hd * HEAD_PAD + (1 - hd % 2) * V_DIM
                v_ref[t, val:val + V_DIM, :] = vt[hd * V_DIM:(hd + 1) * V_DIM, ts]
                v_ref[t, pad:pad + V_DIM, :] = ones

    u = proj(3)
    v = proj(4)
    mu = jnp.mean(v, axis=-1, keepdims=True)
    vc = v - mu
    var = jnp.mean(vc * vc, axis=-1, keepdims=True)
    vn = vc * lax.rsqrt(var + 1e-5) * gv_ref[...] + bv_ref[...]
    if prompt:
        vnb = vn.astype(BF)
        lane2 = lax.broadcasted_iota(jnp.int32, (GMLP_CHUNK, LANE), 1)
        gd = GMLP_WIDTH // GMLP_GROUPS
        for c in range(0, tm // GMLP_CHUNK, 2):
            ra = slice(c * GMLP_CHUNK, (c + 1) * GMLP_CHUNK)
            rb = slice((c + 1) * GMLP_CHUNK, (c + 2) * GMLP_CHUNK)
            for gp in range(GMLP_GROUPS // 2):
                cs = slice(gp * LANE, (gp + 1) * LANE)
                mm = _dot(gm_a_ref[gp], jnp.concatenate([vnb[ra, cs], vnb[rb, cs]], axis=1))
                for rs, ls in ((ra, slice(0, LANE)), (rb, slice(LANE, 2 * LANE))):
                    mixed = jnp.where(lane2 < gd, mm[:GMLP_CHUNK, ls], mm[GMLP_CHUNK:, ls])
                    hb_ref[rs, cs] = (u[rs, cs] * (mixed + gm_b_ref[:, cs])).astype(hb_ref.dtype)
    else:
        vn_ref[...] = vn
        hb_ref[...] = (u * (gm_a_ref[...] * vn + gm_b_ref[...])).astype(hb_ref.dtype)

    oq_ref[...] = proj(5) * (GLA_DK ** -0.5)
    ok_ref[...] = proj(6)
    ov_ref[...] = proj(7).astype(ov_ref.dtype)
    a = _dot(small.astype(BF), wa2_ref[...]) + ba_ref[...]
    lf_ref[...] = (jnp.minimum(a, 0.0) - jnp.log(1.0 + jnp.exp(-jnp.abs(a)))) * (1.0 / GLA_TAU)
    gg = proj(9)
    sg_ref[...] = (gg * jax.nn.sigmoid(gg)).astype(sg_ref.dtype)


def _front(x, ctab, stab, w, l, prompt, seq):
    r = x.shape[0]
    tm = ROW_TILE if prompt else r
    nt = r // tm
    row = lambda n: pl.BlockSpec((tm, n), lambda i: (i, 0))
    rows = lambda n, dt: (row(n), jax.ShapeDtypeStruct((r, n), dt))
    width = MLA_HEADS * HEAD_PAD
    if prompt:
        tab = pl.BlockSpec((tm, LANE), lambda i: (i % (seq // tm), 0))
        wkv_a, wkv_b = w['wk_pad'], w['wuvt']
        gm_a, gm_b = w['ws'], w['bias_tab']
        vt_out = (pl.BlockSpec((tm // ATTN_TILE, width, ATTN_TILE), lambda i: (i, 0, 0)),
                  jax.ShapeDtypeStruct((r // ATTN_TILE, width, ATTN_TILE), BF))
        outs = (rows(width, BF), rows(width, BF), vt_out, rows(KV_RANK, F32), rows(ROPE_DIM, F32),
                rows(GMLP_WIDTH, BF), rows(GLA_KW, F32), rows(GLA_KW, F32), rows(GLA_VW, BF), rows(GLA_KW, F32),
                rows(GLA_VW, BF))
    else:
        tab = pl.BlockSpec((tm, LANE), lambda i: (0, 0))
        wkv_a, wkv_b = w['wukt'], w['wuvt']
        gm_a, gm_b = w['ws00'], w['bs0']
        outs = (rows(width, F32), rows(MLA_HEADS * KV_RANK, F32), rows(KV_RANK, F32), rows(ROPE_DIM, F32),
                rows(GMLP_WIDTH, BF), rows(GMLP_WIDTH, F32), rows(GLA_KW, F32), rows(GLA_KW, F32), rows(GLA_VW, F32),
                rows(GLA_KW, F32), rows(GLA_VW, F32))
    consts = (w['wt'], w['small_t'], w['wq_cat'], wkv_a, wkv_b, w['wa2_pad'], gm_a, gm_b,
              w['g_mix'], w['g_q'], w['g_kv'], w['g_v'], w['b_v'], w['b_a'])
    return pl.pallas_call(
        functools.partial(_front_kernel, prompt, tm),
        grid=(nt,),
        in_specs=[row(D_MODEL), tab, tab] + [_layer_spec(c, l) for c in consts],
        out_specs=[o[0] for o in outs],
        out_shape=[o[1] for o in outs],
        compiler_params=_params(("parallel",)),
        name="front_prompt" if prompt else "front_sample",
    )(x, ctab, stab, *consts)


def _attn_kernel(q_ref, k_ref, vt_ref, o_ref, m_ref, acc_ref, st_ref, p_ref):
    tk, tq = ATTN_TILE, ATTN_Q_TILE
    ratio = tq // tk
    qi = pl.program_id(1)
    m_ref[...] = jnp.full(m_ref.shape, -jnp.inf, F32)
    acc_ref[...] = jnp.zeros(acc_ref.shape, F32)

    def tile(j, diag):
        off = pl.multiple_of(j * tk, tk)
        qs = slice(0 if diag is None else diag * tk, tq)
        for h0 in range(0, MLA_HEADS, ATTN_HEAD_BATCH):
            heads = [(hd, slice(hd * HEAD_PAD, (hd + 1) * HEAD_PAD)) for hd in range(h0, h0 + ATTN_HEAD_BATCH)]
            mx = {}
            for hd, cs in heads:
                st = _dot_nt(k_ref[pl.ds(off, tk), cs], q_ref[qs, cs])
                if diag is not None:
                    w = tq - diag * tk
                    visible = (lax.broadcasted_iota(jnp.int32, (tk, w), 0)
                               <= lax.broadcasted_iota(jnp.int32, (tk, w), 1))
                    st = jnp.where(visible, st, -jnp.inf)
                st_ref[hd, :, qs] = st
                mx[hd] = jnp.broadcast_to(jnp.max(st, axis=0, keepdims=True), (SUB, st.shape[1]))
            alpha = {}
            for hd, cs in heads:
                m = m_ref[hd, :, qs]
                m_new = jnp.maximum(m, mx[hd])
                alpha[hd] = jnp.exp2(m - m_new)
                m_ref[hd, :, qs] = m_new
                w = m_new.shape[1]
                e = jnp.exp2(st_ref[hd, :, qs].reshape(tk // SUB, SUB, w) - m_new[None])
                p_ref[hd, :, qs] = e.reshape(tk, w).astype(BF)
            for hd, cs in heads:
                nrow = V_DIM + ATTN_ONES_ROWS
                r0 = (hd % 2) * (V_DIM - ATTN_ONES_ROWS)
                rs = slice(r0, r0 + nrow)
                vrows = slice(hd * HEAD_PAD + r0, hd * HEAD_PAD + r0 + nrow)
                pv = _dot(vt_ref[j, vrows, :], p_ref[hd, :, qs])
                w = pv.shape[1]
                old = acc_ref[hd, rs, qs].reshape(nrow // SUB, SUB, w) * alpha[hd][None]
                acc_ref[hd, rs, qs] = old.reshape(nrow, w) + pv

    def body(j, carry):
        tile(j, None)
        return carry

    lax.fori_loop(0, qi * ratio, body, 0)
    for d in range(ratio):
        tile(qi * ratio + d, d)
    rowi = lax.broadcasted_iota(jnp.int32, (HEAD_PAD, tq), 0)
    for pr in range(MLA_HEADS // 2):
        ae = acc_ref[2 * pr]
        ao = acc_ref[2 * pr + 1]
        out_t = jnp.where(rowi < V_DIM, ae / ae[V_DIM:V_DIM + 1, :], ao / ao[V_DIM - 1:V_DIM, :])
        o_ref[:, pr * HEAD_PAD:(pr + 1) * HEAD_PAD] = out_t.T.astype(o_ref.dtype)


def _attention(q, k, vt, batch, seq):
    tk, tq = ATTN_TILE, ATTN_Q_TILE
    width = MLA_HEADS * HEAD_PAD
    q3, k3 = (a.reshape(batch, seq, width) for a in (q, k))
    vt4 = vt.reshape(batch, seq // tk, width, tk)
    out = pl.pallas_call(
        _attn_kernel,
        grid=(batch, seq // tq),
        in_specs=[pl.BlockSpec((None, tq, width), lambda b, i: (b, i, 0)),
                  pl.BlockSpec((None, seq, width), lambda b, i: (b, 0, 0)),
                  pl.BlockSpec((None, seq // tk, width, tk), lambda b, i: (b, 0, 0, 0))],
        out_specs=pl.BlockSpec((None, tq, MLA_HEADS * V_DIM), lambda b, i: (b, i, 0)),
        out_shape=jax.ShapeDtypeStruct((batch, seq, MLA_HEADS * V_DIM), BF),
        scratch_shapes=[pltpu.VMEM((MLA_HEADS, SUB, tq), F32), pltpu.VMEM((MLA_HEADS, HEAD_PAD, tq), F32),
                        pltpu.VMEM((MLA_HEADS, tk, tq), F32), pltpu.VMEM((MLA_HEADS, tk, tq), BF)],
        compiler_params=_params(("parallel", "arbitrary")),
        name="mla_prompt_attn",
    )(q3, k3, vt4)
    return out.reshape(batch * seq, MLA_HEADS * V_DIM)


_GLA_LEVELS = (1, 2, 4, 8, 16, 32, 64)


def _gla_tables():
    c = GLA_CHUNK
    t = np.arange(c)[:, None]
    r = np.arange(c)[None, :]
    blocks = []
    for m in _GLA_LEVELS[1:] + (c,):
        blocks.append(((t // m == r // m) & (r <= t)))
    for m in _GLA_LEVELS[1:] + (c,):
        blocks.append(((t // m == r // m) & (r > t)))
    lstack = np.concatenate(blocks, axis=0).astype(np.float32)
    lstack = np.concatenate([lstack, lstack], axis=1)
    masks = [np.eye(c, dtype=np.float32)]
    for m in _GLA_LEVELS:
        masks.append((((t // m) % 2 == 1) & (r // m == t // m - 1)).astype(np.float32))
    return jnp.asarray(lstack, BF), jnp.asarray(np.stack(masks), F32)


def _gla_kernel(nchunk, q_ref, k_ref, v_ref, lf_ref, sg_ref, l_ref, mask_ref, g_ref, hc_ref, s_ref,
                px_ref, qt_ref, kt_ref, z_ref):
    c = GLA_CHUNK
    ci = pl.program_id(1)
    nlev = len(_GLA_LEVELS)
    npair = GLA_HEADS // 2

    @pl.when(ci == 0)
    def _():
        z_ref[...] = jnp.zeros_like(z_ref)

    lf = lf_ref[...]
    hi = lf.astype(BF)
    lo = (lf - hi.astype(F32)).astype(BF)
    px_ref[...] = _dot(l_ref[...], jnp.concatenate([hi, lo], axis=0))
    q = q_ref[...]
    k = k_ref[...]
    lane = lax.broadcasted_iota(jnp.int32, (c, LANE), 1)
    low = lane < GLA_DK

    def put(idx, qv, kv):
        for p in range(npair):
            qp = qv[:, p * LANE:(p + 1) * LANE]
            qt_ref[idx, p, 0:c, :] = jnp.where(low, qp, 0.0).astype(BF)
            qt_ref[idx, p, c:2 * c, :] = jnp.where(low, 0.0, qp).astype(BF)
        kt_ref[idx] = kv.astype(BF)

    pblk = lambda i: px_ref[i * c:(i + 1) * c, :]
    put(0, q, k)
    put(1, q * jnp.exp(lf), k)
    for i in range(nlev - 1):
        put(2 + i, q * jnp.exp(pblk(i)), k * jnp.exp(pblk(nlev + i)))
    bcum = pblk(nlev - 1)
    put(nlev + 1, q * jnp.exp(bcum), k * jnp.exp(pblk(2 * nlev - 1)))
    decay = jnp.exp(bcum[c - 1:c, :])

    for p in range(npair):
        a = jnp.zeros((2 * c, c), F32)
        for lv in range(nlev + 1):
            mk = mask_ref[lv]
            sc = _dot_nt(qt_ref[lv, p], kt_ref[lv, :, p * LANE:(p + 1) * LANE])
            a = a + sc * jnp.concatenate([mk, mk], axis=0)
        kx = kt_ref[nlev + 1, :, p * LANE:(p + 1) * LANE]
        for hh in range(2):
            hd = 2 * p + hh
            vs = slice(hd * GLA_DV, (hd + 1) * GLA_DV)
            vh = v_ref[:, vs]
            z = z_ref[hd]
            o = (_dot(a[hh * c:(hh + 1) * c].astype(BF), vh)
                 + _dot_nt(qt_ref[nlev + 1, p, hh * c:(hh + 1) * c, :], z.astype(BF)))
            y = _rms(o, g_ref[...])
            hc_ref[:, vs] = (y * sg_ref[:, vs].astype(F32)).astype(hc_ref.dtype)
            z_ref[hd] = z * decay[:, p * LANE:(p + 1) * LANE] + _dot_tn(vh, kx)

    @pl.when(ci == nchunk - 1)
    def _():
        for hd in range(GLA_HEADS):
            zt = z_ref[hd].T
            s_ref[hd] = zt[(hd % 2) * GLA_DK:(hd % 2 + 1) * GLA_DK, :]


def _gla(gq, gk, gv, lf, sg, g_gla, l, batch, seq):
    c = GLA_CHUNK
    nchunk = seq // c
    lstack, masks = _gla_tables()
    nlev = len(_GLA_LEVELS)
    r3 = lambda a: a.reshape(batch, seq, a.shape[-1])
    blk = lambda n: pl.BlockSpec((None, c, n), lambda b, i: (b, i, 0))
    hc, s = pl.pallas_call(
        functools.partial(_gla_kernel, nchunk),
        grid=(batch, nchunk),
        in_specs=[blk(GLA_KW), blk(GLA_KW), blk(GLA_VW), blk(GLA_KW), blk(GLA_VW),
                  _const_spec(lstack.shape), _const_spec(masks.shape), _layer_spec(g_gla, l)],
        out_specs=[blk(GLA_VW), pl.BlockSpec((None, GLA_HEADS, GLA_DK, GLA_DV), lambda b, i: (b, 0, 0, 0))],
        out_shape=[jax.ShapeDtypeStruct((batch, seq, GLA_VW), BF),
                   jax.ShapeDtypeStruct((batch, GLA_HEADS, GLA_DK, GLA_DV), F32)],
        scratch_shapes=[pltpu.VMEM((2 * nlev * c, GLA_KW), F32),
                        pltpu.VMEM((nlev + 2, GLA_HEADS // 2, 2 * c, LANE), BF),
                        pltpu.VMEM((nlev + 2, c, GLA_KW), BF),
                        pltpu.VMEM((GLA_HEADS, GLA_DV, LANE), F32)],
        compiler_params=_params(("parallel", "arbitrary")),
        name="gla_prompt",
    )(r3(gq), r3(gk), r3(gv), r3(lf), r3(sg), lstack, masks, g_gla)
    return hc.reshape(batch * seq, GLA_VW), s


def _merge_kernel(x_ref, ha_ref, hb_ref, hc_ref, gmix_ref, wt_ref, wpa_ref, wpb_ref, wpc_ref, wo_ref, o_ref):
    x = x_ref[...]
    h = _rms(x, gmix_ref[...]).astype(BF)
    mix = None
    for j, (hr, wr) in enumerate(((ha_ref, wpa_ref), (hb_ref, wpb_ref), (hc_ref, wpc_ref))):
        g0 = _IN_OFF[10] + j * D_MODEL
        gate = jax.nn.sigmoid(_dot_nt(h, wt_ref[g0:g0 + D_MODEL, :]))
        term = gate * _dot(hr[...].astype(BF), wr[...])
        mix = term if mix is None else mix + term
    o_ref[...] = x + _dot(mix.astype(BF), wo_ref[...])


def _merge(x, ha, hb, hc, w, l, tm):
    r = x.shape[0]
    row = lambda n: pl.BlockSpec((tm, n), lambda i: (i, 0))
    consts = (w['g_mix'], w['wt'], w['wpa'], w['wpb'], w['wpc'], w['wo'])
    return pl.pallas_call(
        _merge_kernel,
        grid=(r // tm,),
        in_specs=[row(D_MODEL), row(ha.shape[1]), row(hb.shape[1]), row(hc.shape[1])]
                 + [_layer_spec(c, l) for c in consts],
        out_specs=row(D_MODEL),
        out_shape=jax.ShapeDtypeStruct((r, D_MODEL), F32),
        compiler_params=_params(("parallel",)),
        name="merge",
    )(x, ha, hb, hc, *consts)


def _ffn_kernel(final, x_ref, g_ref, wgu_ref, wd_ref, gf_ref, o_ref):
    x = x_ref[...]
    hn = _rms(x, g_ref[...]).astype(BF)
    acc = jnp.zeros(x.shape, F32)
    for cidx in range(D_FF // FF_CHUNK):
        g = _dot(hn, wgu_ref[0, cidx])
        u = _dot(hn, wgu_ref[1, cidx])
        act = (g * jax.nn.sigmoid(g) * u).astype(BF)
        acc = acc + _dot(act, wd_ref[cidx])
    y = x + acc
    if final:
        y = _rms(y, gf_ref[...])
    o_ref[...] = y


def _ffn(x, w, l, g_final, final, tm):
    r = x.shape[0]
    row = pl.BlockSpec((tm, D_MODEL), lambda i: (i, 0))
    consts = (w['g_ffn'], w['wgu'], w['wd'], g_final)
    return pl.pallas_call(
        functools.partial(_ffn_kernel, final),
        grid=(r // tm,),
        in_specs=[row] + [_layer_spec(c, l) for c in consts[:-1]] + [_const_spec(g_final.shape)],
        out_specs=row,
        out_shape=jax.ShapeDtypeStruct((r, D_MODEL), F32),
        compiler_params=_params(("parallel",)),
        name="ffn",
    )(x, *consts)


GLA_STEP_BATCH = 8
DECODE_PAGES = 32
DECODE_SLOTS = 3
PAGE_GROUP = 8


def _decode_kernel(layer, nch, pt_ref, ql_ref, qr_ref, cn_ref, kn_ref, ckv_hbm, krt_hbm, o_ref,
                   cbuf, rbuf, cb, rb, sem, m_ref, l_ref, acc_ref):
    npg = DECODE_PAGES
    total = ql_ref.shape[0] * nch

    def page_copies(slot, i, pid):
        rows = pl.ds(i * PAGE_SIZE, PAGE_SIZE)
        return (pltpu.make_async_copy(ckv_hbm.at[layer, pid], cbuf.at[slot, rows, :], sem.at[0, slot]),
                pltpu.make_async_copy(krt_hbm.at[layer, pid], rbuf.at[slot, :, rows], sem.at[1, slot]))

    def start_pages(step, slot, lo, hi):
        b = step // nch
        c = step % nch
        for i in range(lo, hi):
            for cp in page_copies(slot, i, pt_ref[b, c * npg + i]):
                cp.start()

    def wait(slot):
        for i in range(npg):
            for cp in page_copies(slot, i, 0):
                cp.wait()

    for s0 in range(DECODE_SLOTS - 1):
        start_pages(min(s0, total - 1), s0, 0, npg)
    ngroup = npg // PAGE_GROUP
    span = PAGE_GROUP * PAGE_SIZE
    half = KV_RANK // 2

    def body(step, carry):
        slot = step % DECODE_SLOTS
        fill = (step + DECODE_SLOTS - 1) % DECODE_SLOTS
        b = step // nch
        c = step % nch
        nxt = jnp.minimum(step + DECODE_SLOTS - 1, total - 1)
        ql = ql_ref[b]
        qr = qr_ref[b]

        @pl.when(c == 0)
        def _():
            cn = cn_ref[b]
            m_ref[...] = (jnp.sum(ql * cn, axis=-1, keepdims=True) + jnp.sum(qr * kn_ref[b], axis=-1, keepdims=True))
            l_ref[...] = jnp.ones_like(l_ref)
            acc_ref[...] = jnp.broadcast_to(cn, acc_ref.shape)

        wait(slot)
        qlb = ql.astype(BF)
        qrb = qr.astype(BF)
        ss = []
        for g in range(ngroup):
            rows = slice(g * span, (g + 1) * span)
            cb[rows, :] = cbuf[slot, rows, :].astype(BF)
            rb[:, rows] = rbuf[slot, :, rows].astype(BF)
            ss.append(_dot_nt(qlb, cb[rows, :]) + _dot(qrb, rb[:, rows]))
            start_pages(nxt, fill, g * PAGE_GROUP, (g + 1) * PAGE_GROUP)
        parts = []
        for hf in range(2):
            gs = range(hf * ngroup // 2, (hf + 1) * ngroup // 2)
            mh = ss[gs[0]].max(axis=-1, keepdims=True)
            for g in gs[1:]:
                mh = jnp.maximum(mh, ss[g].max(axis=-1, keepdims=True))
            lh = jnp.zeros_like(mh)
            ah = jnp.zeros((MLA_HEADS, KV_RANK), F32)
            for g in gs:
                p = jnp.exp2(ss[g] - mh)
                lh = lh + jnp.sum(p, axis=-1, keepdims=True)
                pb = p.astype(BF)
                rows = slice(g * span, (g + 1) * span)
                ah = ah + jnp.concatenate([_dot(pb, cb[rows, :half]), _dot(pb, cb[rows, half:])], axis=1)
            parts.append((mh, lh, ah))
        m = m_ref[...]
        m_new = jnp.maximum(jnp.maximum(m, parts[0][0]), parts[1][0])
        alpha = jnp.exp2(m - m_new)
        l = alpha * l_ref[...]
        acc = alpha * acc_ref[...]
        for mh, lh, ah in parts:
            wh = jnp.exp2(mh - m_new)
            l = l + wh * lh
            acc = acc + wh * ah
        m_ref[...] = m_new
        l_ref[...] = l
        acc_ref[...] = acc

        @pl.when(c == nch - 1)
        def _():
            o_ref[b] = acc / l

        return carry

    lax.fori_loop(0, total, body, 0)
    for extra in range(DECODE_SLOTS - 1):
        wait((total + extra) % DECODE_SLOTS)


def _decode(layer, page_table, ql, qr, ckv_new, kr_new, cache_ckv, cache_krope_t):
    b, npages = page_table.shape
    npg = DECODE_PAGES
    assert npages % npg == 0 and npg % PAGE_GROUP == 0
    keys = npg * PAGE_SIZE
    vmem = pl.BlockSpec(memory_space=pltpu.VMEM)
    hbm = pl.BlockSpec(memory_space=pl.ANY)
    return pl.pallas_call(
        functools.partial(_decode_kernel, layer, npages // npg),
        in_specs=[pl.BlockSpec(memory_space=pltpu.SMEM), vmem, vmem, vmem, vmem, hbm, hbm],
        out_specs=vmem,
        out_shape=jax.ShapeDtypeStruct((b, MLA_HEADS, KV_RANK), F32),
        scratch_shapes=[pltpu.VMEM((DECODE_SLOTS, keys, KV_RANK), F32), pltpu.VMEM((DECODE_SLOTS, ROPE_DIM, keys), F32),
                        pltpu.VMEM((keys, KV_RANK), BF), pltpu.VMEM((ROPE_DIM, keys), BF),
                        pltpu.SemaphoreType.DMA((2, DECODE_SLOTS)),
                        pltpu.VMEM((MLA_HEADS, 1), F32), pltpu.VMEM((MLA_HEADS, 1), F32),
                        pltpu.VMEM((MLA_HEADS, KV_RANK), F32)],
        compiler_params=pltpu.CompilerParams(vmem_limit_bytes=VMEM_LIMIT),
        name="mla_decode",
    )(page_table, ql, qr, ckv_new, kr_new, cache_ckv, cache_krope_t)


def _sample_ha_kernel(ol_ref, w_ref, o_ref):
    acc = None
    for hd in range(MLA_HEADS):
        t = _dot(ol_ref[hd].astype(BF), w_ref[hd])
        acc = t if acc is None else acc + t
    o_ref[...] = acc


def _sample_ha(o_lat_t, wuv_wide, l):
    b = o_lat_t.shape[1]
    return pl.pallas_call(
        _sample_ha_kernel,
        grid=(1,),
        in_specs=[_const_spec(o_lat_t.shape), _layer_spec(wuv_wide, l)],
        out_specs=pl.BlockSpec((b, MLA_HEADS * V_DIM), lambda i: (0, 0)),
        out_shape=jax.ShapeDtypeStruct((b, MLA_HEADS * V_DIM), F32),
        compiler_params=_params(("arbitrary",)),
        name="sample_ha",
    )(o_lat_t, wuv_wide)


def _gla_step_kernel(q_ref, k_ref, v_ref, lf_ref, sg_ref, g_ref, s0_ref, hc_ref, s_ref):
    dk = GLA_DK
    eye = lax.broadcasted_iota(jnp.int32, (dk, dk), 0) == lax.broadcasted_iota(jnp.int32, (dk, dk), 1)
    col = lambda rowv: jnp.sum(jnp.where(eye, jnp.broadcast_to(rowv, (dk, dk)), 0.0), axis=-1, keepdims=True)
    for bi in range(q_ref.shape[0]):
        rw = slice(bi, bi + 1)
        for hd in range(GLA_HEADS):
            ks = slice(hd * dk, (hd + 1) * dk)
            vs = slice(hd * GLA_DV, (hd + 1) * GLA_DV)
            s_new = jnp.exp(col(lf_ref[rw, ks])) * s0_ref[bi, hd] + col(k_ref[rw, ks]) * v_ref[rw, vs]
            s_ref[bi, hd] = s_new
            o = jnp.sum(col(q_ref[rw, ks]) * s_new, axis=0, keepdims=True)
            hc_ref[rw, vs] = _rms(o, g_ref[...]) * sg_ref[rw, vs]


def _gla_step(gq, gk, gv, lf, sg, g_gla, state_all, l):
    b = gq.shape[0]
    gb = GLA_STEP_BATCH
    assert b % gb == 0
    blk = lambda n: pl.BlockSpec((gb, n), lambda i: (i, 0))
    sshape = (GLA_HEADS, GLA_DK, GLA_DV)
    return pl.pallas_call(
        _gla_step_kernel,
        grid=(b // gb,),
        in_specs=[blk(GLA_KW), blk(GLA_KW), blk(GLA_VW), blk(GLA_KW), blk(GLA_VW), _layer_spec(g_gla, l),
                  pl.BlockSpec((None, gb) + sshape, lambda i: (l, i, 0, 0, 0))],
        out_specs=[blk(GLA_VW), pl.BlockSpec((gb,) + sshape, lambda i: (i, 0, 0, 0))],
        out_shape=[jax.ShapeDtypeStruct((b, GLA_VW), F32), jax.ShapeDtypeStruct((b,) + sshape, F32)],
        compiler_params=_params(("parallel",)),
        name="gla_step",
    )(gq, gk, gv, lf, sg, g_gla, state_all)


def kernel(x_prompt, x_sample, cache_ckv, cache_krope, state_gla, page_table, g_mix, w_in, g_q, w_uq, g_kv, w_uk, w_uv, g_v, b_v, w_s, b_s, w_a2, b_a, g_gla, w_pa, w_pb, w_pc, w_o, g_ffn, w_gu, w_down, g_final):
    batch, seq, _ = x_prompt.shape
    dec_b, dec_t, _ = x_sample.shape
    assert dec_t == 1 and seq % ROW_TILE == 0
    depth = w_in.shape[0]
    past = page_table.shape[1] * PAGE_SIZE
    ct_p, st_p = _rope_tables(jnp.arange(seq, dtype=F32))
    ct_s, st_s = _rope_tables(jnp.full((dec_b,), float(past), F32))
    gfin = g_final.reshape(1, -1).astype(F32)
    cache_krope_t = jnp.swapaxes(cache_krope, 2, 3)

    xp = x_prompt.reshape(batch * seq, D_MODEL)
    xs = x_sample.reshape(dec_b, D_MODEL)
    w = _prep_weights(g_mix, w_in, g_q, w_uq, g_kv, w_uk, w_uv, g_v, b_v, w_s, b_s, w_a2, b_a, g_gla,
                      w_pa, w_pb, w_pc, w_o, g_ffn, w_gu, w_down)
    ckv_p, kr_p, gla_p, ckv_s, kr_s, gla_s, gv_s = [], [], [], [], [], [], []
    for l in range(depth):
        last = l == depth - 1
        q, k, v, ckv, kr, hb, gq, gk, gv, lf, sg = _front(xp, ct_p, st_p, w, l, True, seq)
        ha = _attention(q, k, v, batch, seq)
        hc, s_c = _gla(gq, gk, gv, lf, sg, w['g_gla'], l, batch, seq)
        xp = _ffn(_merge(xp, ha, hb, hc, w, l, ROW_TILE), w, l, gfin, last, ROW_TILE)
        ckv_p.append(ckv.reshape(batch, seq, KV_RANK))
        kr_p.append(kr.reshape(batch, seq, ROPE_DIM))
        gla_p.append(s_c)
        q, ql, ckv, kr, hb, vn, gq, gk, gv, lf, sg = _front(xs, ct_s, st_s, w, l, False, seq)
        qr = q.reshape(dec_b, MLA_HEADS, HEAD_PAD)[:, :, NOPE_DIM:NOPE_DIM + ROPE_DIM]
        o_lat = _decode(l, page_table, ql.reshape(dec_b, MLA_HEADS, KV_RANK), qr,
                        ckv.reshape(dec_b, 1, KV_RANK), kr.reshape(dec_b, 1, ROPE_DIM), cache_ckv, cache_krope_t)
        ha = _sample_ha(o_lat.transpose(1, 0, 2), w['wuv_wide'], l)
        hc, s_c = _gla_step(gq, gk, gv, lf, sg, w['g_gla'], state_gla, l)
        xs = _ffn(_merge(xs, ha, hb, hc, w, l, dec_b), w, l, gfin, last, dec_b)
        ckv_s.append(ckv.reshape(dec_b, 1, KV_RANK))
        kr_s.append(kr.reshape(dec_b, 1, ROPE_DIM))
        gla_s.append(s_c)
        gv_s.append(vn.reshape(dec_b, 1, GMLP_WIDTH))
    return (xp.reshape(batch, seq, D_MODEL), xs.reshape(dec_b, 1, D_MODEL), jnp.stack(ckv_p), jnp.stack(kr_p),
            jnp.stack(gla_p), jnp.stack(ckv_s), jnp.stack(kr_s), jnp.stack(gla_s), jnp.stack(gv_s))
```

```python
import functools
import math

import numpy as np
import jax
import jax.numpy as jnp
from jax import lax
from jax.experimental import pallas as pl
from jax.experimental.pallas import tpu as pltpu

D_MODEL = 1024
PAGE_SIZE = 128
MLA_HEADS = 8
Q_RANK = 384
KV_RANK = 256
NOPE_DIM = 64
ROPE_DIM = 32
V_DIM = 64
ROPE_THETA = 10000.0
GMLP_GROUPS = 8
GMLP_CHUNK = 128
GMLP_WIDTH = 512
GLA_HEADS = 4
GLA_DK = 64
GLA_DV = 128
GLA_KW = GLA_HEADS * GLA_DK
GLA_VW = GLA_HEADS * GLA_DV
GLA_GATE_RANK = 16
GLA_TAU = 16.0
N_BRANCH = 3
D_FF = 2816
IN_SPLITS = (Q_RANK, KV_RANK, ROPE_DIM, GMLP_WIDTH, GMLP_WIDTH, GLA_KW, GLA_KW, GLA_VW,
             GLA_GATE_RANK, GLA_VW, N_BRANCH * D_MODEL)
_IN_OFF = tuple(int(v) for v in np.cumsum((0,) + IN_SPLITS))

LANE = 128
SUB = 8
HEAD_PAD = 128
GLA_CHUNK = 128
ROW_TILE = 512
ATTN_TILE = 256
ATTN_Q_TILE = 512
ATTN_HEAD_BATCH = 8
ATTN_ONES_ROWS = 16
FF_CHUNK = 256
VMEM_LIMIT = 56 * 1024 * 1024

BF = jnp.bfloat16
F32 = jnp.float32


def _dot(a, b):
    return jnp.dot(a, b, preferred_element_type=F32)


def _dot_nt(a, b):
    return lax.dot_general(a, b, (((1,), (1,)), ((), ())), preferred_element_type=F32)


def _dot_tn(a, b):
    return lax.dot_general(a, b, (((0,), (0,)), ((), ())), preferred_element_type=F32)


def _rms(x, g, eps=1e-6):
    return x * lax.rsqrt(jnp.mean(x * x, axis=-1, keepdims=True) + eps) * g


def _const_spec(shape):
    nd = len(shape)
    return pl.BlockSpec(shape, lambda *_: (0,) * nd, pipeline_mode=pl.Buffered(1))


def _layer_spec(arr, l):
    nd = arr.ndim - 1
    return pl.BlockSpec((None,) + arr.shape[1:], lambda *_: (l,) + (0,) * nd, pipeline_mode=pl.Buffered(1))


def _params(sem):
    return pltpu.CompilerParams(dimension_semantics=sem, vmem_limit_bytes=VMEM_LIMIT)


def _prep_weights(g_mix, w_in, g_q, w_uq, g_kv, w_uk, w_uv, g_v, b_v, w_s, b_s, w_a2, b_a, g_gla,
                  w_pa, w_pb, w_pc, w_o, g_ffn, w_gu, w_down):
    d = w_in.shape[0]
    half = ROPE_DIM // 2
    wt = jnp.swapaxes(w_in, 1, 2).astype(BF)
    kr = wt[:, _IN_OFF[2]:_IN_OFF[3]]
    kr_swap = jnp.concatenate([-kr[:, half:], kr[:, :half]], axis=1)
    z16 = jnp.zeros((d, 16, D_MODEL), BF)
    z32 = jnp.zeros((d, 32, D_MODEL), BF)
    small_t = jnp.concatenate([kr_swap, wt[:, _IN_OFF[8]:_IN_OFF[9]], z16, kr, z32], axis=1)

    qn, qr = w_uq[..., :NOPE_DIM], w_uq[..., NOPE_DIM:]
    qr_swap = jnp.concatenate([-qr[..., half:], qr[..., :half]], axis=-1)
    wq_cat = jnp.concatenate([qn, qr, qr_swap], axis=-1).reshape(d, Q_RANK, MLA_HEADS * HEAD_PAD).astype(BF)

    wk_pad = jnp.concatenate([w_uk, jnp.zeros_like(w_uk)], axis=-1).reshape(d, KV_RANK, MLA_HEADS * HEAD_PAD).astype(BF)
    wuvt = w_uv.reshape(d, KV_RANK, MLA_HEADS * V_DIM).transpose(0, 2, 1).astype(BF)
    wukt = jnp.concatenate([w_uk.transpose(0, 2, 3, 1),
                            jnp.zeros((d, MLA_HEADS, HEAD_PAD - NOPE_DIM, KV_RANK), F32)], axis=2).astype(BF)
    eye_h = jnp.eye(MLA_HEADS, dtype=F32)
    wuv_wide = jnp.einsum('lrhd,hg->lhrgd', w_uv, eye_h).reshape(d, MLA_HEADS, KV_RANK, MLA_HEADS * V_DIM).astype(BF)

    wa2_pad = jnp.zeros((d, LANE, GLA_KW), F32).at[:, 32:32 + GLA_GATE_RANK].set(w_a2).astype(BF)
    tri = jnp.tril(jnp.ones((GMLP_CHUNK, GMLP_CHUNK), bool))
    ws = jnp.where(tri[None, None], w_s, 0.0).astype(BF).reshape(d, GMLP_GROUPS // 2, 2 * GMLP_CHUNK, GMLP_CHUNK)
    gd = GMLP_WIDTH // GMLP_GROUPS
    bias_tab = jnp.repeat(b_s.transpose(0, 2, 1), gd, axis=2)
    ws00 = jnp.repeat(w_s[:, :, 0, 0], gd, axis=1)[:, None, :]
    bs0 = jnp.repeat(b_s[:, :, 0], gd, axis=1)[:, None, :]

    nff = D_FF // FF_CHUNK
    wgu = jnp.swapaxes(jnp.swapaxes(w_gu, 1, 2).reshape(d, 2, nff, FF_CHUNK, D_MODEL), 3, 4).astype(BF)
    wd = w_down.reshape(d, nff, FF_CHUNK, D_MODEL).astype(BF)
    row = lambda a: a.reshape(d, 1, -1).astype(F32)
    return dict(
        wt=wt, small_t=small_t, wgu=wgu, wq_cat=wq_cat, wk_pad=wk_pad, wuvt=wuvt, wukt=wukt,
        wuv_wide=wuv_wide, wa2_pad=wa2_pad, ws=ws, bias_tab=bias_tab, ws00=ws00, bs0=bs0,
        g_mix=row(g_mix), g_q=row(g_q), g_kv=row(g_kv), g_v=row(g_v), b_v=row(b_v),
        b_a=row(b_a), g_gla=row(g_gla), g_ffn=row(g_ffn),
        wpa=w_pa.astype(BF), wpb=w_pb.astype(BF), wpc=w_pc.astype(BF), wo=w_o.astype(BF),
        wd=wd)


def _rope_tables(pos):
    half = ROPE_DIM // 2
    inv = jnp.power(ROPE_THETA, -jnp.arange(half, dtype=F32) * 2.0 / ROPE_DIM)
    ang = pos[:, None] * inv[None, :]
    cos, sin = jnp.cos(ang), jnp.sin(ang)
    t = pos.shape[0]
    ctab = jnp.concatenate([jnp.ones((t, NOPE_DIM), F32), cos, cos, jnp.zeros((t, 32), F32)], axis=1)
    stab = jnp.concatenate([jnp.zeros((t, NOPE_DIM), F32), sin, sin, jnp.zeros((t, 32), F32)], axis=1)
    return ctab, stab


def _front_kernel(prompt, tm, x_ref, ct_ref, st_ref, wt_ref, small_ref, wq_cat_ref, wkv_a_ref, wkv_b_ref,
                  wa2_ref, gm_a_ref, gm_b_ref, gmix_ref, gq_ref, gkv_ref, gv_ref, bv_ref, ba_ref, *outs):
    if prompt:
        (q_ref, k_ref, v_ref, ckv_ref, kr_ref, hb_ref, oq_ref, ok_ref, ov_ref, lf_ref, sg_ref) = outs
    else:
        (q_ref, ql_ref, ckv_ref, kr_ref, hb_ref, vn_ref, oq_ref, ok_ref, ov_ref, lf_ref, sg_ref) = outs
    x = x_ref[...]
    h = _rms(x, gmix_ref[...]).astype(BF)
    proj = lambda i: _dot_nt(h, wt_ref[_IN_OFF[i]:_IN_OFF[i + 1], :])
    ct = ct_ref[...]
    st = st_ref[...]
    scale = (NOPE_DIM + ROPE_DIM) ** -0.5 * math.log2(math.e)

    cq = _rms(proj(0), gq_ref[...]).astype(BF)
    qa = _dot(cq, wq_cat_ref[...])
    for hd in range(MLA_HEADS):
        sl = slice(hd * HEAD_PAD, (hd + 1) * HEAD_PAD)
        qh = (qa[:, sl] * ct + pltpu.roll(qa[:, sl], HEAD_PAD - ROPE_DIM, 1) * st) * scale
        q_ref[:, sl] = qh.astype(q_ref.dtype)
        if not prompt:
            ql_ref[:, hd * KV_RANK:(hd + 1) * KV_RANK] = _dot(qh.astype(BF), wkv_a_ref[hd])

    ckv = _rms(proj(1), gkv_ref[...])
    ckv_ref[...] = ckv
    small = _dot_nt(h, small_ref[...])
    lane = lax.broadcasted_iota(jnp.int32, small.shape, 1)
    kr_rot = jnp.where((lane >= NOPE_DIM) & (lane < NOPE_DIM + ROPE_DIM),
                       small * ct + pltpu.roll(small, NOPE_DIM, 1) * st, 0.0)
    if prompt:
        kr_ref[...] = kr_rot.T[NOPE_DIM:NOPE_DIM + ROPE_DIM, :]
    else:
        kr_ref[...] = pltpu.roll(kr_rot, NOPE_DIM, 1)[:, :ROPE_DIM]
    if prompt:
        ckvb = ckv.astype(BF)
        kn = _dot(ckvb, wkv_a_ref[...])
        for hd in range(MLA_HEADS):
            sl = slice(hd * HEAD_PAD, (hd + 1) * HEAD_PAD)
            k_ref[:, sl] = (kn[:, sl] + kr_rot).astype(BF)
        vt = _dot_nt(wkv_b_ref[...], ckvb).astype(BF)
        ones = jnp.ones((V_DIM, ATTN_TILE), BF)
        for t in range(tm // ATTN_TILE):
            ts = slice(t * ATTN_TILE, (t + 1) * ATTN_TILE)
            for hd in range(MLA_HEADS):
                val = hd * HEAD_PAD + (hd % 2) * V_DIM
                pad = hd * HEAD_PAD + (1 - hd % 2) * V_DIM
                v_ref[t, val:val + V_DIM, :] = vt[hd * V_DIM:(hd + 1) * V_DIM, ts]
                v_ref[t, pad:pad + V_DIM, :] = ones

    u = proj(3)
    v = proj(4)
    mu = jnp.mean(v, axis=-1, keepdims=True)
    vc = v - mu
    var = jnp.mean(vc * vc, axis=-1, keepdims=True)
    vn = vc * lax.rsqrt(var + 1e-5) * gv_ref[...] + bv_ref[...]
    if prompt:
        vnb = vn.astype(BF)
        lane2 = lax.broadcasted_iota(jnp.int32, (GMLP_CHUNK, LANE), 1)
        gd = GMLP_WIDTH // GMLP_GROUPS
        for c in range(0, tm // GMLP_CHUNK, 2):
            ra = slice(c * GMLP_CHUNK, (c + 1) * GMLP_CHUNK)
            rb = slice((c + 1) * GMLP_CHUNK, (c + 2) * GMLP_CHUNK)
            for gp in range(GMLP_GROUPS // 2):
                cs = slice(gp * LANE, (gp + 1) * LANE)
                mm = _dot(gm_a_ref[gp], jnp.concatenate([vnb[ra, cs], vnb[rb, cs]], axis=1))
                for rs, ls in ((ra, slice(0, LANE)), (rb, slice(LANE, 2 * LANE))):
                    mixed = jnp.where(lane2 < gd, mm[:GMLP_CHUNK, ls], mm[GMLP_CHUNK:, ls])
                    hb_ref[rs, cs] = (u[rs, cs] * (mixed + gm_b_ref[:, cs])).astype(hb_ref.dtype)
    else:
        vn_ref[...] = vn
        hb_ref[...] = (u * (gm_a_ref[...] * vn + gm_b_ref[...])).astype(hb_ref.dtype)

    oq_ref[...] = proj(5) * (GLA_DK ** -0.5)
    ok_ref[...] = proj(6)
    ov_ref[...] = proj(7).astype(ov_ref.dtype)
    a = _dot(small.astype(BF), wa2_ref[...]) + ba_ref[...]
    lf_ref[...] = (jnp.minimum(a, 0.0) - jnp.log(1.0 + jnp.exp(-jnp.abs(a)))) * (1.0 / GLA_TAU)
    gg = proj(9)
    sg_ref[...] = (gg * jax.nn.sigmoid(gg)).astype(sg_ref.dtype)


def _front(x, ctab, stab, w, l, prompt, seq):
    r = x.shape[0]
    tm = ROW_TILE if prompt else r
    nt = r // tm
    row = lambda n: pl.BlockSpec((tm, n), lambda i: (i, 0))
    rows = lambda n, dt: (row(n), jax.ShapeDtypeStruct((r, n), dt))
    width = MLA_HEADS * HEAD_PAD
    if prompt:
        tab = pl.BlockSpec((tm, LANE), lambda i: (i % (seq // tm), 0))
        wkv_a, wkv_b = w['wk_pad'], w['wuvt']
        gm_a, gm_b = w['ws'], w['bias_tab']
        vt_out = (pl.BlockSpec((tm // ATTN_TILE, width, ATTN_TILE), lambda i: (i, 0, 0)),
                  jax.ShapeDtypeStruct((r // ATTN_TILE, width, ATTN_TILE), BF))
        nst = seq // tm
        krt_out = (pl.BlockSpec((None, ROPE_DIM, tm), lambda i: (i // nst, 0, i % nst)),
                   jax.ShapeDtypeStruct((r // seq, ROPE_DIM, seq), F32))
        outs = (rows(width, BF), rows(width, BF), vt_out, rows(KV_RANK, F32), krt_out,
                rows(GMLP_WIDTH, BF), rows(GLA_KW, F32), rows(GLA_KW, F32), rows(GLA_VW, BF), rows(GLA_KW, F32),
                rows(GLA_VW, BF))
    else:
        tab = pl.BlockSpec((tm, LANE), lambda i: (0, 0))
        wkv_a, wkv_b = w['wukt'], w['wuvt']
        gm_a, gm_b = w['ws00'], w['bs0']
        outs = (rows(width, F32), rows(MLA_HEADS * KV_RANK, F32), rows(KV_RANK, F32), rows(ROPE_DIM, F32),
                rows(GMLP_WIDTH, BF), rows(GMLP_WIDTH, F32), rows(GLA_KW, F32), rows(GLA_KW, F32), rows(GLA_VW, F32),
                rows(GLA_KW, F32), rows(GLA_VW, F32))
    consts = (w['wt'], w['small_t'], w['wq_cat'], wkv_a, wkv_b, w['wa2_pad'], gm_a, gm_b,
              w['g_mix'], w['g_q'], w['g_kv'], w['g_v'], w['b_v'], w['b_a'])
    return pl.pallas_call(
        functools.partial(_front_kernel, prompt, tm),
        grid=(nt,),
        in_specs=[row(D_MODEL), tab, tab] + [_layer_spec(c, l) for c in consts],
        out_specs=[o[0] for o in outs],
        out_shape=[o[1] for o in outs],
        compiler_params=_params(("parallel",)),
        name="front_prompt" if prompt else "front_sample",
    )(x, ctab, stab, *consts)


def _attn_kernel(q_ref, k_ref, vt_ref, o_ref, m_ref, acc_ref, st_ref, p_ref):
    tk, tq = ATTN_TILE, ATTN_Q_TILE
    ratio = tq // tk
    qi = pl.program_id(1)
    m_ref[...] = jnp.full(m_ref.shape, -jnp.inf, F32)
    acc_ref[...] = jnp.zeros(acc_ref.shape, F32)

    def tile(j, diag):
        off = pl.multiple_of(j * tk, tk)
        qs = slice(0 if diag is None else diag * tk, tq)
        for h0 in range(0, MLA_HEADS, ATTN_HEAD_BATCH):
            heads = [(hd, slice(hd * HEAD_PAD, (hd + 1) * HEAD_PAD)) for hd in range(h0, h0 + ATTN_HEAD_BATCH)]
            mx = {}
            for hd, cs in heads:
                st = _dot_nt(k_ref[pl.ds(off, tk), cs], q_ref[qs, cs])
                if diag is not None:
                    w = tq - diag * tk
                    visible = (lax.broadcasted_iota(jnp.int32, (tk, w), 0)
                               <= lax.broadcasted_iota(jnp.int32, (tk, w), 1))
                    st = jnp.where(visible, st, -jnp.inf)
                st_ref[hd, :, qs] = st
                mx[hd] = jnp.broadcast_to(jnp.max(st, axis=0, keepdims=True), (SUB, st.shape[1]))
            alpha = {}
            for hd, cs in heads:
                m = m_ref[hd, :, qs]
                m_new = jnp.maximum(m, mx[hd])
                alpha[hd] = jnp.exp2(m - m_new)
                m_ref[hd, :, qs] = m_new
                w = m_new.shape[1]
                e = jnp.exp2(st_ref[hd, :, qs].reshape(tk // SUB, SUB, w) - m_new[None])
                p_ref[hd, :, qs] = e.reshape(tk, w).astype(BF)
            for hd, cs in heads:
                nrow = V_DIM + ATTN_ONES_ROWS
                r0 = (hd % 2) * (V_DIM - ATTN_ONES_ROWS)
                rs = slice(r0, r0 + nrow)
                vrows = slice(hd * HEAD_PAD + r0, hd * HEAD_PAD + r0 + nrow)
                pv = _dot(vt_ref[j, vrows, :], p_ref[hd, :, qs])
                w = pv.shape[1]
                old = acc_ref[hd, rs, qs].reshape(nrow // SUB, SUB, w) * alpha[hd][None]
                acc_ref[hd, rs, qs] = old.reshape(nrow, w) + pv

    def body(j, carry):
        tile(j, None)
        return carry

    lax.fori_loop(0, qi * ratio, body, 0)
    for d in range(ratio):
        tile(qi * ratio + d, d)
    rowi = lax.broadcasted_iota(jnp.int32, (HEAD_PAD, tq), 0)
    for pr in range(MLA_HEADS // 2):
        ae = acc_ref[2 * pr]
        ao = acc_ref[2 * pr + 1]
        out_t = jnp.where(rowi < V_DIM, ae / ae[V_DIM:V_DIM + 1, :], ao / ao[V_DIM - 1:V_DIM, :])
        o_ref[:, pr * HEAD_PAD:(pr + 1) * HEAD_PAD] = out_t.T.astype(o_ref.dtype)


def _attention(q, k, vt, batch, seq):
    tk, tq = ATTN_TILE, ATTN_Q_TILE
    width = MLA_HEADS * HEAD_PAD
    q3, k3 = (a.reshape(batch, seq, width) for a in (q, k))
    vt4 = vt.reshape(batch, seq // tk, width, tk)
    out = pl.pallas_call(
        _attn_kernel,
        grid=(batch, seq // tq),
        in_specs=[pl.BlockSpec((None, tq, width), lambda b, i: (b, i, 0)),
                  pl.BlockSpec((None, seq, width), lambda b, i: (b, 0, 0)),
                  pl.BlockSpec((None, seq // tk, width, tk), lambda b, i: (b, 0, 0, 0))],
        out_specs=pl.BlockSpec((None, tq, MLA_HEADS * V_DIM), lambda b, i: (b, i, 0)),
        out_shape=jax.ShapeDtypeStruct((batch, seq, MLA_HEADS * V_DIM), BF),
        scratch_shapes=[pltpu.VMEM((MLA_HEADS, SUB, tq), F32), pltpu.VMEM((MLA_HEADS, HEAD_PAD, tq), F32),
                        pltpu.VMEM((MLA_HEADS, tk, tq), F32), pltpu.VMEM((MLA_HEADS, tk, tq), BF)],
        compiler_params=_params(("parallel", "arbitrary")),
        name="mla_prompt_attn",
    )(q3, k3, vt4)
    return out.reshape(batch * seq, MLA_HEADS * V_DIM)


_GLA_LEVELS = (1, 2, 4, 8, 16, 32, 64)


def _gla_tables():
    c = GLA_CHUNK
    t = np.arange(c)[:, None]
    r = np.arange(c)[None, :]
    blocks = []
    for m in _GLA_LEVELS[1:] + (c,):
        blocks.append(((t // m == r // m) & (r <= t)))
    for m in _GLA_LEVELS[1:] + (c,):
        blocks.append(((t // m == r // m) & (r > t)))
    lstack = np.concatenate(blocks, axis=0).astype(np.float32)
    lstack = np.concatenate([lstack, lstack], axis=1)
    masks = [np.eye(c, dtype=np.float32)]
    for m in _GLA_LEVELS:
        masks.append((((t // m) % 2 == 1) & (r // m == t // m - 1)).astype(np.float32))
    return jnp.asarray(lstack, BF), jnp.asarray(np.stack(masks), F32)


def _gla_kernel(nchunk, q_ref, k_ref, v_ref, lf_ref, sg_ref, l_ref, mask_ref, g_ref, hc_ref, s_ref,
                px_ref, qt_ref, kt_ref, z_ref):
    c = GLA_CHUNK
    ci = pl.program_id(1)
    nlev = len(_GLA_LEVELS)
    npair = GLA_HEADS // 2

    @pl.when(ci == 0)
    def _():
        z_ref[...] = jnp.zeros_like(z_ref)

    lf = lf_ref[...]
    hi = lf.astype(BF)
    lo = (lf - hi.astype(F32)).astype(BF)
    hilo = jnp.concatenate([hi, lo], axis=0)
    for p in range(npair):
        ls = slice(p * LANE, (p + 1) * LANE)
        px_ref[:, ls] = _dot(l_ref[...], hilo[:, ls])
    q = q_ref[...]
    k = k_ref[...]
    lane = lax.broadcasted_iota(jnp.int32, (c, LANE), 1)
    low = lane < GLA_DK

    def put(idx, qv, kv):
        for p in range(npair):
            qp = qv[:, p * LANE:(p + 1) * LANE]
            qt_ref[idx, p, 0:c, :] = jnp.where(low, qp, 0.0).astype(BF)
            qt_ref[idx, p, c:2 * c, :] = jnp.where(low, 0.0, qp).astype(BF)
        kt_ref[idx] = kv.astype(BF)

    pblk = lambda i: px_ref[i * c:(i + 1) * c, :]
    put(0, q, k)
    put(1, q * jnp.exp(lf), k)
    for i in range(nlev - 1):
        put(2 + i, q * jnp.exp(pblk(i)), k * jnp.exp(pblk(nlev + i)))
    bcum = pblk(nlev - 1)
    put(nlev + 1, q * jnp.exp(bcum), k * jnp.exp(pblk(2 * nlev - 1)))
    decay = jnp.exp(bcum[c - 1:c, :])

    for p in range(npair):
        a = jnp.zeros((2 * c, c), F32)
        for lv in range(nlev + 1):
            mk = mask_ref[lv]
            sc = _dot_nt(qt_ref[lv, p], kt_ref[lv, :, p * LANE:(p + 1) * LANE])
            a = a + sc * jnp.concatenate([mk, mk], axis=0)
        kx = kt_ref[nlev + 1, :, p * LANE:(p + 1) * LANE]
        for hh in range(2):
            hd = 2 * p + hh
            vs = slice(hd * GLA_DV, (hd + 1) * GLA_DV)
            vh = v_ref[:, vs]
            z = z_ref[hd]
            o = (_dot(a[hh * c:(hh + 1) * c].astype(BF), vh)
                 + _dot_nt(qt_ref[nlev + 1, p, hh * c:(hh + 1) * c, :], z.astype(BF)))
            y = _rms(o, g_ref[...])
            hc_ref[:, vs] = (y * sg_ref[:, vs].astype(F32)).astype(hc_ref.dtype)
            z_ref[hd] = z * decay[:, p * LANE:(p + 1) * LANE] + _dot_tn(vh, kx)

    @pl.when(ci == nchunk - 1)
    def _():
        for hd in range(GLA_HEADS):
            zt = z_ref[hd].T
            s_ref[hd] = zt[(hd % 2) * GLA_DK:(hd % 2 + 1) * GLA_DK, :]


def _gla(gq, gk, gv, lf, sg, g_gla, l, batch, seq):
    c = GLA_CHUNK
    nchunk = seq // c
    lstack, masks = _gla_tables()
    nlev = len(_GLA_LEVELS)
    r3 = lambda a: a.reshape(batch, seq, a.shape[-1])
    blk = lambda n: pl.BlockSpec((None, c, n), lambda b, i: (b, i, 0))
    hc, s = pl.pallas_call(
        functools.partial(_gla_kernel, nchunk),
        grid=(batch, nchunk),
        in_specs=[blk(GLA_KW), blk(GLA_KW), blk(GLA_VW), blk(GLA_KW), blk(GLA_VW),
                  _const_spec(lstack.shape), _const_spec(masks.shape), _layer_spec(g_gla, l)],
        out_specs=[blk(GLA_VW), pl.BlockSpec((None, GLA_HEADS, GLA_DK, GLA_DV), lambda b, i: (b, 0, 0, 0))],
        out_shape=[jax.ShapeDtypeStruct((batch, seq, GLA_VW), BF),
                   jax.ShapeDtypeStruct((batch, GLA_HEADS, GLA_DK, GLA_DV), F32)],
        scratch_shapes=[pltpu.VMEM((2 * nlev * c, GLA_KW), F32),
                        pltpu.VMEM((nlev + 2, GLA_HEADS // 2, 2 * c, LANE), BF),
                        pltpu.VMEM((nlev + 2, c, GLA_KW), BF),
                        pltpu.VMEM((GLA_HEADS, GLA_DV, LANE), F32)],
        compiler_params=_params(("parallel", "arbitrary")),
        name="gla_prompt",
    )(r3(gq), r3(gk), r3(gv), r3(lf), r3(sg), lstack, masks, g_gla)
    return hc.reshape(batch * seq, GLA_VW), s


def _merge_kernel(x_ref, ha_ref, hb_ref, hc_ref, gmix_ref, wt_ref, wpa_ref, wpb_ref, wpc_ref, wo_ref, o_ref):
    x = x_ref[...]
    h = _rms(x, gmix_ref[...]).astype(BF)
    mix = None
    for j, (hr, wr) in enumerate(((ha_ref, wpa_ref), (hb_ref, wpb_ref), (hc_ref, wpc_ref))):
        g0 = _IN_OFF[10] + j * D_MODEL
        gate = jax.nn.sigmoid(_dot_nt(h, wt_ref[g0:g0 + D_MODEL, :]))
        term = gate * _dot(hr[...].astype(BF), wr[...])
        mix = term if mix is None else mix + term
    o_ref[...] = x + _dot(mix.astype(BF), wo_ref[...])


def _merge(x, ha, hb, hc, w, l, tm):
    r = x.shape[0]
    row = lambda n: pl.BlockSpec((tm, n), lambda i: (i, 0))
    consts = (w['g_mix'], w['wt'], w['wpa'], w['wpb'], w['wpc'], w['wo'])
    return pl.pallas_call(
        _merge_kernel,
        grid=(r // tm,),
        in_specs=[row(D_MODEL), row(ha.shape[1]), row(hb.shape[1]), row(hc.shape[1])]
                 + [_layer_spec(c, l) for c in consts],
        out_specs=row(D_MODEL),
        out_shape=jax.ShapeDtypeStruct((r, D_MODEL), F32),
        compiler_params=_params(("parallel",)),
        name="merge",
    )(x, ha, hb, hc, *consts)


def _ffn_kernel(final, x_ref, g_ref, wgu_ref, wd_ref, gf_ref, o_ref):
    x = x_ref[...]
    hn = _rms(x, g_ref[...]).astype(BF)
    acc = jnp.zeros(x.shape, F32)
    for cidx in range(D_FF // FF_CHUNK):
        g = _dot(hn, wgu_ref[0, cidx])
        u = _dot(hn, wgu_ref[1, cidx])
        act = (g * jax.nn.sigmoid(g) * u).astype(BF)
        acc = acc + _dot(act, wd_ref[cidx])
    y = x + acc
    if final:
        y = _rms(y, gf_ref[...])
    o_ref[...] = y


def _ffn(x, w, l, g_final, final, tm):
    r = x.shape[0]
    row = pl.BlockSpec((tm, D_MODEL), lambda i: (i, 0))
    consts = (w['g_ffn'], w['wgu'], w['wd'], g_final)
    return pl.pallas_call(
        functools.partial(_ffn_kernel, final),
        grid=(r // tm,),
        in_specs=[row] + [_layer_spec(c, l) for c in consts[:-1]] + [_const_spec(g_final.shape)],
        out_specs=row,
        out_shape=jax.ShapeDtypeStruct((r, D_MODEL), F32),
        compiler_params=_params(("parallel",)),
        name="ffn",
    )(x, *consts)


GLA_STEP_BATCH = 8
DECODE_PAGES = 32
DECODE_SLOTS = 3
PAGE_GROUP = 8


def _decode_kernel(layer, nch, pt_ref, ql_ref, qr_ref, cn_ref, kn_ref, ckv_hbm, krt_hbm, o_ref,
                   cbuf, rbuf, cb, rb, sem, m_ref, l_ref, acc_ref):
    npg = DECODE_PAGES
    total = ql_ref.shape[0] * nch

    def page_copies(slot, i, pid):
        rows = pl.ds(i * PAGE_SIZE, PAGE_SIZE)
        return (pltpu.make_async_copy(ckv_hbm.at[layer, pid], cbuf.at[slot, rows, :], sem.at[0, slot]),
                pltpu.make_async_copy(krt_hbm.at[layer, pid], rbuf.at[slot, :, rows], sem.at[1, slot]))

    def start_pages(step, slot, lo, hi):
        b = step // nch
        c = step % nch
        for i in range(lo, hi):
            for cp in page_copies(slot, i, pt_ref[b, c * npg + i]):
                cp.start()

    def wait(slot):
        for i in range(npg):
            for cp in page_copies(slot, i, 0):
                cp.wait()

    for s0 in range(DECODE_SLOTS - 1):
        start_pages(min(s0, total - 1), s0, 0, npg)
    ngroup = npg // PAGE_GROUP
    span = PAGE_GROUP * PAGE_SIZE
    half = KV_RANK // 2

    def body(step, carry):
        slot = step % DECODE_SLOTS
        fill = (step + DECODE_SLOTS - 1) % DECODE_SLOTS
        b = step // nch
        c = step % nch
        nxt = jnp.minimum(step + DECODE_SLOTS - 1, total - 1)
        ql = ql_ref[b]
        qr = qr_ref[b]

        @pl.when(c == 0)
        def _():
            cn = cn_ref[b]
            m_ref[...] = (jnp.sum(ql * cn, axis=-1, keepdims=True) + jnp.sum(qr * kn_ref[b], axis=-1, keepdims=True))
            l_ref[...] = jnp.ones_like(l_ref)
            acc_ref[...] = jnp.broadcast_to(cn, acc_ref.shape)

        wait(slot)
        qlb = ql.astype(BF)
        qrb = qr.astype(BF)
        ss = []
        for g in range(ngroup):
            rows = slice(g * span, (g + 1) * span)
            cb[rows, :] = cbuf[slot, rows, :].astype(BF)
            rb[:, rows] = rbuf[slot, :, rows].astype(BF)
            ss.append(_dot_nt(qlb, cb[rows, :]) + _dot(qrb, rb[:, rows]))
            start_pages(nxt, fill, g * PAGE_GROUP, (g + 1) * PAGE_GROUP)
        parts = []
        for hf in range(2):
            gs = range(hf * ngroup // 2, (hf + 1) * ngroup // 2)
            mh = ss[gs[0]].max(axis=-1, keepdims=True)
            for g in gs[1:]:
                mh = jnp.maximum(mh, ss[g].max(axis=-1, keepdims=True))
            lh = jnp.zeros_like(mh)
            ah = jnp.zeros((MLA_HEADS, KV_RANK), F32)
            for g in gs:
                p = jnp.exp2(ss[g] - mh)
                lh = lh + jnp.sum(p, axis=-1, keepdims=True)
                pb = p.astype(BF)
                rows = slice(g * span, (g + 1) * span)
                ah = ah + jnp.concatenate([_dot(pb, cb[rows, :half]), _dot(pb, cb[rows, half:])], axis=1)
            parts.append((mh, lh, ah))
        m = m_ref[...]
        m_new = jnp.maximum(jnp.maximum(m, parts[0][0]), parts[1][0])
        alpha = jnp.exp2(m - m_new)
        l = alpha * l_ref[...]
        acc = alpha * acc_ref[...]
        for mh, lh, ah in parts:
            wh = jnp.exp2(mh - m_new)
            l = l + wh * lh
            acc = acc + wh * ah
        m_ref[...] = m_new
        l_ref[...] = l
        acc_ref[...] = acc

        @pl.when(c == nch - 1)
        def _():
            o_ref[b] = acc / l

        return carry

    lax.fori_loop(0, total, body, 0)
    for extra in range(DECODE_SLOTS - 1):
        wait((total + extra) % DECODE_SLOTS)


def _decode(layer, page_table, ql, qr, ckv_new, kr_new, cache_ckv, cache_krope_t):
    b, npages = page_table.shape
    npg = DECODE_PAGES
    assert npages % npg == 0 and npg % PAGE_GROUP == 0
    keys = npg * PAGE_SIZE
    vmem = pl.BlockSpec(memory_space=pltpu.VMEM)
    hbm = pl.BlockSpec(memory_space=pl.ANY)
    return pl.pallas_call(
        functools.partial(_decode_kernel, layer, npages // npg),
        in_specs=[pl.BlockSpec(memory_space=pltpu.SMEM), vmem, vmem, vmem, vmem, hbm, hbm],
        out_specs=vmem,
        out_shape=jax.ShapeDtypeStruct((b, MLA_HEADS, KV_RANK), F32),
        scratch_shapes=[pltpu.VMEM((DECODE_SLOTS, keys, KV_RANK), F32), pltpu.VMEM((DECODE_SLOTS, ROPE_DIM, keys), F32),
                        pltpu.VMEM((keys, KV_RANK), BF), pltpu.VMEM((ROPE_DIM, keys), BF),
                        pltpu.SemaphoreType.DMA((2, DECODE_SLOTS)),
                        pltpu.VMEM((MLA_HEADS, 1), F32), pltpu.VMEM((MLA_HEADS, 1), F32),
                        pltpu.VMEM((MLA_HEADS, KV_RANK), F32)],
        compiler_params=pltpu.CompilerParams(vmem_limit_bytes=VMEM_LIMIT),
        name="mla_decode",
    )(page_table, ql, qr, ckv_new, kr_new, cache_ckv, cache_krope_t)


def _sample_ha_kernel(ol_ref, w_ref, o_ref):
    acc = None
    for hd in range(MLA_HEADS):
        t = _dot(ol_ref[hd].astype(BF), w_ref[hd])
        acc = t if acc is None else acc + t
    o_ref[...] = acc


def _sample_ha(o_lat_t, wuv_wide, l):
    b = o_lat_t.shape[1]
    return pl.pallas_call(
        _sample_ha_kernel,
        grid=(1,),
        in_specs=[_const_spec(o_lat_t.shape), _layer_spec(wuv_wide, l)],
        out_specs=pl.BlockSpec((b, MLA_HEADS * V_DIM), lambda i: (0, 0)),
        out_shape=jax.ShapeDtypeStruct((b, MLA_HEADS * V_DIM), F32),
        compiler_params=_params(("arbitrary",)),
        name="sample_ha",
    )(o_lat_t, wuv_wide)


def _gla_step_kernel(q_ref, k_ref, v_ref, lf_ref, sg_ref, g_ref, s0_ref, hc_ref, s_ref):
    dk = GLA_DK
    eye = lax.broadcasted_iota(jnp.int32, (dk, dk), 0) == lax.broadcasted_iota(jnp.int32, (dk, dk), 1)
    col = lambda rowv: jnp.sum(jnp.where(eye, jnp.broadcast_to(rowv, (dk, dk)), 0.0), axis=-1, keepdims=True)
    for bi in range(q_ref.shape[0]):
        rw = slice(bi, bi + 1)
        for hd in range(GLA_HEADS):
            ks = slice(hd * dk, (hd + 1) * dk)
            vs = slice(hd * GLA_DV, (hd + 1) * GLA_DV)
            s_new = jnp.exp(col(lf_ref[rw, ks])) * s0_ref[bi, hd] + col(k_ref[rw, ks]) * v_ref[rw, vs]
            s_ref[bi, hd] = s_new
            o = jnp.sum(col(q_ref[rw, ks]) * s_new, axis=0, keepdims=True)
            hc_ref[rw, vs] = _rms(o, g_ref[...]) * sg_ref[rw, vs]


def _gla_step(gq, gk, gv, lf, sg, g_gla, state_all, l):
    b = gq.shape[0]
    gb = GLA_STEP_BATCH
    assert b % gb == 0
    blk = lambda n: pl.BlockSpec((gb, n), lambda i: (i, 0))
    sshape = (GLA_HEADS, GLA_DK, GLA_DV)
    return pl.pallas_call(
        _gla_step_kernel,
        grid=(b // gb,),
        in_specs=[blk(GLA_KW), blk(GLA_KW), blk(GLA_VW), blk(GLA_KW), blk(GLA_VW), _layer_spec(g_gla, l),
                  pl.BlockSpec((None, gb) + sshape, lambda i: (l, i, 0, 0, 0))],
        out_specs=[blk(GLA_VW), pl.BlockSpec((gb,) + sshape, lambda i: (i, 0, 0, 0))],
        out_shape=[jax.ShapeDtypeStruct((b, GLA_VW), F32), jax.ShapeDtypeStruct((b,) + sshape, F32)],
        compiler_params=_params(("parallel",)),
        name="gla_step",
    )(gq, gk, gv, lf, sg, g_gla, state_all)


def kernel(x_prompt, x_sample, cache_ckv, cache_krope, state_gla, page_table, g_mix, w_in, g_q, w_uq, g_kv, w_uk, w_uv, g_v, b_v, w_s, b_s, w_a2, b_a, g_gla, w_pa, w_pb, w_pc, w_o, g_ffn, w_gu, w_down, g_final):
    batch, seq, _ = x_prompt.shape
    dec_b, dec_t, _ = x_sample.shape
    assert dec_t == 1 and seq % ROW_TILE == 0
    depth = w_in.shape[0]
    past = page_table.shape[1] * PAGE_SIZE
    ct_p, st_p = _rope_tables(jnp.arange(seq, dtype=F32))
    ct_s, st_s = _rope_tables(jnp.full((dec_b,), float(past), F32))
    gfin = g_final.reshape(1, -1).astype(F32)
    cache_krope_t = jnp.swapaxes(cache_krope, 2, 3)

    xp = x_prompt.reshape(batch * seq, D_MODEL)
    xs = x_sample.reshape(dec_b, D_MODEL)
    w = _prep_weights(g_mix, w_in, g_q, w_uq, g_kv, w_uk, w_uv, g_v, b_v, w_s, b_s, w_a2, b_a, g_gla,
                      w_pa, w_pb, w_pc, w_o, g_ffn, w_gu, w_down)
    ckv_p, kr_p, gla_p, ckv_s, kr_s, gla_s, gv_s = [], [], [], [], [], [], []
    for l in range(depth):
        last = l == depth - 1
        q, k, v, ckv, kr, hb, gq, gk, gv, lf, sg = _front(xp, ct_p, st_p, w, l, True, seq)
        ha = _attention(q, k, v, batch, seq)
        hc, s_c = _gla(gq, gk, gv, lf, sg, w['g_gla'], l, batch, seq)
        xp = _ffn(_merge(xp, ha, hb, hc, w, l, ROW_TILE), w, l, gfin, last, ROW_TILE)
        ckv_p.append(ckv.reshape(batch, seq, KV_RANK))
        kr_p.append(kr)
        gla_p.append(s_c)
        q, ql, ckv, kr, hb, vn, gq, gk, gv, lf, sg = _front(xs, ct_s, st_s, w, l, False, seq)
        qr = q.reshape(dec_b, MLA_HEADS, HEAD_PAD)[:, :, NOPE_DIM:NOPE_DIM + ROPE_DIM]
        o_lat = _decode(l, page_table, ql.reshape(dec_b, MLA_HEADS, KV_RANK), qr,
                        ckv.reshape(dec_b, 1, KV_RANK), kr.reshape(dec_b, 1, ROPE_DIM), cache_ckv, cache_krope_t)
        ha = _sample_ha(o_lat.transpose(1, 0, 2), w['wuv_wide'], l)
        hc, s_c = _gla_step(gq, gk, gv, lf, sg, w['g_gla'], state_gla, l)
        xs = _ffn(_merge(xs, ha, hb, hc, w, l, dec_b), w, l, gfin, last, dec_b)
        ckv_s.append(ckv.reshape(dec_b, 1, KV_RANK))
        kr_s.append(kr.reshape(dec_b, 1, ROPE_DIM))
        gla_s.append(s_c)
        gv_s.append(vn.reshape(dec_b, 1, GMLP_WIDTH))
    return (xp.reshape(batch, seq, D_MODEL), xs.reshape(dec_b, 1, D_MODEL), jnp.stack(ckv_p), jnp.swapaxes(jnp.stack(kr_p), 2, 3),
            jnp.stack(gla_p), jnp.stack(ckv_s), jnp.stack(kr_s), jnp.stack(gla_s), jnp.stack(gv_s))
```

```python
import functools
import math

import numpy as np
import jax
import jax.numpy as jnp
from jax import lax
from jax.experimental import pallas as pl
from jax.experimental.pallas import tpu as pltpu

D_MODEL = 1024
PAGE_SIZE = 128
MLA_HEADS = 8
Q_RANK = 384
KV_RANK = 256
NOPE_DIM = 64
ROPE_DIM = 32
V_DIM = 64
ROPE_THETA = 10000.0
GMLP_GROUPS = 8
GMLP_CHUNK = 128
GMLP_WIDTH = 512
GLA_HEADS = 4
GLA_DK = 64
GLA_DV = 128
GLA_KW = GLA_HEADS * GLA_DK
GLA_VW = GLA_HEADS * GLA_DV
GLA_GATE_RANK = 16
GLA_TAU = 16.0
N_BRANCH = 3
D_FF = 2816
IN_SPLITS = (Q_RANK, KV_RANK, ROPE_DIM, GMLP_WIDTH, GMLP_WIDTH, GLA_KW, GLA_KW, GLA_VW,
             GLA_GATE_RANK, GLA_VW, N_BRANCH * D_MODEL)
_IN_OFF = tuple(int(v) for v in np.cumsum((0,) + IN_SPLITS))

LANE = 128
SUB = 8
HEAD_PAD = 128
GLA_CHUNK = 128
ROW_TILE = 512
ATTN_TILE = 256
ATTN_Q_TILE = 512
ATTN_HEAD_BATCH = 8
ATTN_ONES_ROWS = 16
FF_CHUNK = 256
VMEM_LIMIT = 56 * 1024 * 1024

BF = jnp.bfloat16
F32 = jnp.float32


def _dot(a, b):
    return jnp.dot(a, b, preferred_element_type=F32)


def _dot_nt(a, b):
    return lax.dot_general(a, b, (((1,), (1,)), ((), ())), preferred_element_type=F32)


def _dot_tn(a, b):
    return lax.dot_general(a, b, (((0,), (0,)), ((), ())), preferred_element_type=F32)


def _rms(x, g, eps=1e-6):
    return x * lax.rsqrt(jnp.mean(x * x, axis=-1, keepdims=True) + eps) * g


def _const_spec(shape):
    nd = len(shape)
    return pl.BlockSpec(shape, lambda *_: (0,) * nd, pipeline_mode=pl.Buffered(1))


def _layer_spec(arr, l):
    nd = arr.ndim - 1
    return pl.BlockSpec((None,) + arr.shape[1:], lambda *_: (l,) + (0,) * nd, pipeline_mode=pl.Buffered(1))


def _params(sem):
    return pltpu.CompilerParams(dimension_semantics=sem, vmem_limit_bytes=VMEM_LIMIT)


def _prep_weights(g_mix, w_in, g_q, w_uq, g_kv, w_uk, w_uv, g_v, b_v, w_s, b_s, w_a2, b_a, g_gla,
                  w_pa, w_pb, w_pc, w_o, g_ffn, w_gu, w_down):
    d = w_in.shape[0]
    half = ROPE_DIM // 2
    wt = jnp.swapaxes(w_in, 1, 2).astype(BF)
    kr = wt[:, _IN_OFF[2]:_IN_OFF[3]]
    kr_swap = jnp.concatenate([-kr[:, half:], kr[:, :half]], axis=1)
    z16 = jnp.zeros((d, 16, D_MODEL), BF)
    z32 = jnp.zeros((d, 32, D_MODEL), BF)
    small_t = jnp.concatenate([kr_swap, wt[:, _IN_OFF[8]:_IN_OFF[9]], z16, kr, z32], axis=1)

    qn, qr = w_uq[..., :NOPE_DIM], w_uq[..., NOPE_DIM:]
    qr_swap = jnp.concatenate([-qr[..., half:], qr[..., :half]], axis=-1)
    wq_cat = jnp.concatenate([qn, qr, qr_swap], axis=-1).reshape(d, Q_RANK, MLA_HEADS * HEAD_PAD).astype(BF)

    wk_pad = jnp.concatenate([w_uk, jnp.zeros_like(w_uk)], axis=-1).reshape(d, KV_RANK, MLA_HEADS * HEAD_PAD).astype(BF)
    wuvt = w_uv.reshape(d, KV_RANK, MLA_HEADS * V_DIM).transpose(0, 2, 1).astype(BF)
    wukt = jnp.concatenate([w_uk.transpose(0, 2, 3, 1),
                            jnp.zeros((d, MLA_HEADS, HEAD_PAD - NOPE_DIM, KV_RANK), F32)], axis=2).astype(BF)
    eye_h = jnp.eye(MLA_HEADS, dtype=F32)
    wuv_wide = jnp.einsum('lrhd,hg->lhrgd', w_uv, eye_h).reshape(d, MLA_HEADS, KV_RANK, MLA_HEADS * V_DIM).astype(BF)

    wa2_pad = jnp.zeros((d, LANE, GLA_KW), F32).at[:, 32:32 + GLA_GATE_RANK].set(w_a2).astype(BF)
    tri = jnp.tril(jnp.ones((GMLP_CHUNK, GMLP_CHUNK), bool))
    ws = jnp.where(tri[None, None], w_s, 0.0).astype(BF).reshape(d, GMLP_GROUPS // 2, 2 * GMLP_CHUNK, GMLP_CHUNK)
    gd = GMLP_WIDTH // GMLP_GROUPS
    bias_tab = jnp.repeat(b_s.transpose(0, 2, 1), gd, axis=2)
    ws00 = jnp.repeat(w_s[:, :, 0, 0], gd, axis=1)[:, None, :]
    bs0 = jnp.repeat(b_s[:, :, 0], gd, axis=1)[:, None, :]

    nff = D_FF // FF_CHUNK
    wgu = jnp.swapaxes(jnp.swapaxes(w_gu, 1, 2).reshape(d, 2, nff, FF_CHUNK, D_MODEL), 3, 4).astype(BF)
    wd = w_down.reshape(d, nff, FF_CHUNK, D_MODEL).astype(BF)
    row = lambda a: a.reshape(d, 1, -1).astype(F32)
    return dict(
        wt=wt, small_t=small_t, wgu=wgu, wq_cat=wq_cat, wk_pad=wk_pad, wuvt=wuvt, wukt=wukt,
        wuv_wide=wuv_wide, wa2_pad=wa2_pad, ws=ws, bias_tab=bias_tab, ws00=ws00, bs0=bs0,
        g_mix=row(g_mix), g_q=row(g_q), g_kv=row(g_kv), g_v=row(g_v), b_v=row(b_v),
        b_a=row(b_a), g_gla=row(g_gla), g_ffn=row(g_ffn),
        wpa=w_pa.astype(BF), wpb=w_pb.astype(BF), wpc=w_pc.astype(BF), wo=w_o.astype(BF),
        wd=wd)


def _rope_tables(pos):
    half = ROPE_DIM // 2
    inv = jnp.power(ROPE_THETA, -jnp.arange(half, dtype=F32) * 2.0 / ROPE_DIM)
    ang = pos[:, None] * inv[None, :]
    cos, sin = jnp.cos(ang), jnp.sin(ang)
    t = pos.shape[0]
    ctab = jnp.concatenate([jnp.ones((t, NOPE_DIM), F32), cos, cos, jnp.zeros((t, 32), F32)], axis=1)
    stab = jnp.concatenate([jnp.zeros((t, NOPE_DIM), F32), sin, sin, jnp.zeros((t, 32), F32)], axis=1)
    return ctab, stab


def _front_kernel(prompt, tm, x_ref, ct_ref, st_ref, wt_ref, small_ref, wq_cat_ref, wkv_a_ref, wkv_b_ref,
                  wa2_ref, gm_a_ref, gm_b_ref, gmix_ref, gq_ref, gkv_ref, gv_ref, bv_ref, ba_ref, *outs):
    if prompt:
        (q_ref, k_ref, v_ref, ckv_ref, kr_ref, hb_ref, oq_ref, ok_ref, ov_ref, lf_ref, sg_ref) = outs
    else:
        (q_ref, ql_ref, ckv_ref, kr_ref, hb_ref, vn_ref, oq_ref, ok_ref, ov_ref, lf_ref, sg_ref) = outs
    x = x_ref[...]
    h = _rms(x, gmix_ref[...]).astype(BF)
    proj = lambda i, j: _dot_nt(h, wt_ref[_IN_OFF[i]:_IN_OFF[j], :])
    z_qkv = proj(0, 2)
    z_uv = proj(3, 5)
    z_gla = proj(5, 8)
    ct = ct_ref[...]
    st = st_ref[...]
    scale = (NOPE_DIM + ROPE_DIM) ** -0.5 * math.log2(math.e)

    cq = _rms(z_qkv[:, :Q_RANK], gq_ref[...]).astype(BF)
    qa = _dot(cq, wq_cat_ref[...])
    for hd in range(MLA_HEADS):
        sl = slice(hd * HEAD_PAD, (hd + 1) * HEAD_PAD)
        qh = (qa[:, sl] * ct + pltpu.roll(qa[:, sl], HEAD_PAD - ROPE_DIM, 1) * st) * scale
        q_ref[:, sl] = qh.astype(q_ref.dtype)
        if not prompt:
            ql_ref[:, hd * KV_RANK:(hd + 1) * KV_RANK] = _dot(qh.astype(BF), wkv_a_ref[hd])

    ckv = _rms(z_qkv[:, Q_RANK:], gkv_ref[...])
    ckv_ref[...] = ckv
    small = _dot_nt(h, small_ref[...])
    lane = lax.broadcasted_iota(jnp.int32, small.shape, 1)
    kr_rot = jnp.where((lane >= NOPE_DIM) & (lane < NOPE_DIM + ROPE_DIM),
                       small * ct + pltpu.roll(small, NOPE_DIM, 1) * st, 0.0)
    if prompt:
        kr_ref[...] = kr_rot.T[NOPE_DIM:NOPE_DIM + ROPE_DIM, :]
    else:
        kr_ref[...] = pltpu.roll(kr_rot, NOPE_DIM, 1)[:, :ROPE_DIM]
    if prompt:
        ckvb = ckv.astype(BF)
        kn = _dot(ckvb, wkv_a_ref[...])
        for hd in range(MLA_HEADS):
            sl = slice(hd * HEAD_PAD, (hd + 1) * HEAD_PAD)
            k_ref[:, sl] = (kn[:, sl] + kr_rot).astype(BF)
        vt = _dot_nt(wkv_b_ref[...], ckvb).astype(BF)
        ones = jnp.ones((V_DIM, ATTN_TILE), BF)
        for t in range(tm // ATTN_TILE):
            ts = slice(t * ATTN_TILE, (t + 1) * ATTN_TILE)
            for hd in range(MLA_HEADS):
                val = hd * HEAD_PAD + (hd % 2) * V_DIM
                pad = hd * HEAD_PAD + (1 - hd % 2) * V_DIM
                v_ref[t, val:val + V_DIM, :] = vt[hd * V_DIM:(hd + 1) * V_DIM, ts]
                v_ref[t, pad:pad + V_DIM, :] = ones

    u = z_uv[:, :GMLP_WIDTH]
    v = z_uv[:, GMLP_WIDTH:]
    mu = jnp.mean(v, axis=-1, keepdims=True)
    vc = v - mu
    var = jnp.mean(vc * vc, axis=-1, keepdims=True)
    vn = vc * lax.rsqrt(var + 1e-5) * gv_ref[...] + bv_ref[...]
    if prompt:
        vnb = vn.astype(BF)
        lane2 = lax.broadcasted_iota(jnp.int32, (GMLP_CHUNK, LANE), 1)
        gd = GMLP_WIDTH // GMLP_GROUPS
        for c in range(0, tm // GMLP_CHUNK, 2):
            ra = slice(c * GMLP_CHUNK, (c + 1) * GMLP_CHUNK)
            rb = slice((c + 1) * GMLP_CHUNK, (c + 2) * GMLP_CHUNK)
            for gp in range(GMLP_GROUPS // 2):
                cs = slice(gp * LANE, (gp + 1) * LANE)
                mm = _dot(gm_a_ref[gp], jnp.concatenate([vnb[ra, cs], vnb[rb, cs]], axis=1))
                for rs, ls in ((ra, slice(0, LANE)), (rb, slice(LANE, 2 * LANE))):
                    mixed = jnp.where(lane2 < gd, mm[:GMLP_CHUNK, ls], mm[GMLP_CHUNK:, ls])
                    hb_ref[rs, cs] = (u[rs, cs] * (mixed + gm_b_ref[:, cs])).astype(hb_ref.dtype)
    else:
        vn_ref[...] = vn
        hb_ref[...] = (u * (gm_a_ref[...] * vn + gm_b_ref[...])).astype(hb_ref.dtype)

    oq_ref[...] = z_gla[:, :GLA_KW] * (GLA_DK ** -0.5)
    ok_ref[...] = z_gla[:, GLA_KW:2 * GLA_KW]
    ov_ref[...] = z_gla[:, 2 * GLA_KW:].astype(ov_ref.dtype)
    a = _dot(small.astype(BF), wa2_ref[...]) + ba_ref[...]
    lf_ref[...] = (jnp.minimum(a, 0.0) - jnp.log(1.0 + jnp.exp(-jnp.abs(a)))) * (1.0 / GLA_TAU)
    gg = proj(9, 10)
    sg_ref[...] = (gg * jax.nn.sigmoid(gg)).astype(sg_ref.dtype)


def _front(x, ctab, stab, w, l, prompt, seq):
    r = x.shape[0]
    tm = ROW_TILE if prompt else r
    nt = r // tm
    row = lambda n: pl.BlockSpec((tm, n), lambda i: (i, 0))
    rows = lambda n, dt: (row(n), jax.ShapeDtypeStruct((r, n), dt))
    width = MLA_HEADS * HEAD_PAD
    if prompt:
        tab = pl.BlockSpec((tm, LANE), lambda i: (i % (seq // tm), 0))
        wkv_a, wkv_b = w['wk_pad'], w['wuvt']
        gm_a, gm_b = w['ws'], w['bias_tab']
        vt_out = (pl.BlockSpec((tm // ATTN_TILE, width, ATTN_TILE), lambda i: (i, 0, 0)),
                  jax.ShapeDtypeStruct((r // ATTN_TILE, width, ATTN_TILE), BF))
        nst = seq // tm
        krt_out = (pl.BlockSpec((None, ROPE_DIM, tm), lambda i: (i // nst, 0, i % nst)),
                   jax.ShapeDtypeStruct((r // seq, ROPE_DIM, seq), F32))
        outs = (rows(width, BF), rows(width, BF), vt_out, rows(KV_RANK, F32), krt_out,
                rows(GMLP_WIDTH, BF), rows(GLA_KW, F32), rows(GLA_KW, F32), rows(GLA_VW, BF), rows(GLA_KW, F32),
                rows(GLA_VW, BF))
    else:
        tab = pl.BlockSpec((tm, LANE), lambda i: (0, 0))
        wkv_a, wkv_b = w['wukt'], w['wuvt']
        gm_a, gm_b = w['ws00'], w['bs0']
        outs = (rows(width, F32), rows(MLA_HEADS * KV_RANK, F32), rows(KV_RANK, F32), rows(ROPE_DIM, F32),
                rows(GMLP_WIDTH, BF), rows(GMLP_WIDTH, F32), rows(GLA_KW, F32), rows(GLA_KW, F32), rows(GLA_VW, F32),
                rows(GLA_KW, F32), rows(GLA_VW, F32))
    consts = (w['wt'], w['small_t'], w['wq_cat'], wkv_a, wkv_b, w['wa2_pad'], gm_a, gm_b,
              w['g_mix'], w['g_q'], w['g_kv'], w['g_v'], w['b_v'], w['b_a'])
    return pl.pallas_call(
        functools.partial(_front_kernel, prompt, tm),
        grid=(nt,),
        in_specs=[row(D_MODEL), tab, tab] + [_layer_spec(c, l) for c in consts],
        out_specs=[o[0] for o in outs],
        out_shape=[o[1] for o in outs],
        compiler_params=_params(("parallel",)),
        name="front_prompt" if prompt else "front_sample",
    )(x, ctab, stab, *consts)


def _attn_kernel(q_ref, k_ref, vt_ref, o_ref, m_ref, acc_ref, st_ref, p_ref):
    tk, tq = ATTN_TILE, ATTN_Q_TILE
    ratio = tq // tk
    qi = pl.program_id(1)
    m_ref[...] = jnp.full(m_ref.shape, -jnp.inf, F32)
    acc_ref[...] = jnp.zeros(acc_ref.shape, F32)

    def tile(j, diag):
        off = pl.multiple_of(j * tk, tk)
        qs = slice(0 if diag is None else diag * tk, tq)
        for h0 in range(0, MLA_HEADS, ATTN_HEAD_BATCH):
            heads = [(hd, slice(hd * HEAD_PAD, (hd + 1) * HEAD_PAD)) for hd in range(h0, h0 + ATTN_HEAD_BATCH)]
            mx = {}
            for hd, cs in heads:
                st = _dot_nt(k_ref[pl.ds(off, tk), cs], q_ref[qs, cs])
                if diag is not None:
                    w = tq - diag * tk
                    visible = (lax.broadcasted_iota(jnp.int32, (tk, w), 0)
                               <= lax.broadcasted_iota(jnp.int32, (tk, w), 1))
                    st = jnp.where(visible, st, -jnp.inf)
                st_ref[hd, :, qs] = st
                mx[hd] = jnp.broadcast_to(jnp.max(st, axis=0, keepdims=True), (SUB, st.shape[1]))
            alpha = {}
            for hd, cs in heads:
                m = m_ref[hd, :, qs]
                m_new = jnp.maximum(m, mx[hd])
                alpha[hd] = jnp.exp2(m - m_new)
                m_ref[hd, :, qs] = m_new
                w = m_new.shape[1]
                e = jnp.exp2(st_ref[hd, :, qs].reshape(tk // SUB, SUB, w) - m_new[None])
                p_ref[hd, :, qs] = e.reshape(tk, w).astype(BF)
            for hd, cs in heads:
                nrow = V_DIM + ATTN_ONES_ROWS
                r0 = (hd % 2) * (V_DIM - ATTN_ONES_ROWS)
                rs = slice(r0, r0 + nrow)
                vrows = slice(hd * HEAD_PAD + r0, hd * HEAD_PAD + r0 + nrow)
                pv = _dot(vt_ref[j, vrows, :], p_ref[hd, :, qs])
                w = pv.shape[1]
                old = acc_ref[hd, rs, qs].reshape(nrow // SUB, SUB, w) * alpha[hd][None]
                acc_ref[hd, rs, qs] = old.reshape(nrow, w) + pv

    def body(j, carry):
        tile(j, None)
        return carry

    lax.fori_loop(0, qi * ratio, body, 0)
    for d in range(ratio):
        tile(qi * ratio + d, d)
    rowi = lax.broadcasted_iota(jnp.int32, (HEAD_PAD, tq), 0)
    for pr in range(MLA_HEADS // 2):
        ae = acc_ref[2 * pr]
        ao = acc_ref[2 * pr + 1]
        out_t = jnp.where(rowi < V_DIM, ae / ae[V_DIM:V_DIM + 1, :], ao / ao[V_DIM - 1:V_DIM, :])
        o_ref[:, pr * HEAD_PAD:(pr + 1) * HEAD_PAD] = out_t.T.astype(o_ref.dtype)


def _attention(q, k, vt, batch, seq):
    tk, tq = ATTN_TILE, ATTN_Q_TILE
    width = MLA_HEADS * HEAD_PAD
    q3, k3 = (a.reshape(batch, seq, width) for a in (q, k))
    vt4 = vt.reshape(batch, seq // tk, width, tk)
    out = pl.pallas_call(
        _attn_kernel,
        grid=(batch, seq // tq),
        in_specs=[pl.BlockSpec((None, tq, width), lambda b, i: (b, i, 0)),
                  pl.BlockSpec((None, seq, width), lambda b, i: (b, 0, 0)),
                  pl.BlockSpec((None, seq // tk, width, tk), lambda b, i: (b, 0, 0, 0))],
        out_specs=pl.BlockSpec((None, tq, MLA_HEADS * V_DIM), lambda b, i: (b, i, 0)),
        out_shape=jax.ShapeDtypeStruct((batch, seq, MLA_HEADS * V_DIM), BF),
        scratch_shapes=[pltpu.VMEM((MLA_HEADS, SUB, tq), F32), pltpu.VMEM((MLA_HEADS, HEAD_PAD, tq), F32),
                        pltpu.VMEM((MLA_HEADS, tk, tq), F32), pltpu.VMEM((MLA_HEADS, tk, tq), BF)],
        compiler_params=_params(("parallel", "arbitrary")),
        name="mla_prompt_attn",
    )(q3, k3, vt4)
    return out.reshape(batch * seq, MLA_HEADS * V_DIM)


_GLA_LEVELS = (1, 2, 4, 8, 16, 32, 64)


def _gla_tables():
    c = GLA_CHUNK
    t = np.arange(c)[:, None]
    r = np.arange(c)[None, :]
    blocks = []
    for m in _GLA_LEVELS[1:] + (c,):
        blocks.append(((t // m == r // m) & (r <= t)))
    for m in _GLA_LEVELS[1:] + (c,):
        blocks.append(((t // m == r // m) & (r > t)))
    lstack = np.concatenate(blocks, axis=0).astype(np.float32)
    lstack = np.concatenate([lstack, lstack], axis=1)
    masks = [np.eye(c, dtype=np.float32)]
    for m in _GLA_LEVELS:
        masks.append((((t // m) % 2 == 1) & (r // m == t // m - 1)).astype(np.float32))
    return jnp.asarray(lstack, BF), jnp.asarray(np.stack(masks), F32)


def _gla_kernel(nchunk, q_ref, k_ref, v_ref, lf_ref, sg_ref, l_ref, mask_ref, g_ref, hc_ref, s_ref,
                px_ref, qt_ref, kt_ref, z_ref):
    c = GLA_CHUNK
    ci = pl.program_id(1)
    nlev = len(_GLA_LEVELS)
    npair = GLA_HEADS // 2

    @pl.when(ci == 0)
    def _():
        z_ref[...] = jnp.zeros_like(z_ref)

    lf = lf_ref[...]
    hi = lf.astype(BF)
    lo = (lf - hi.astype(F32)).astype(BF)
    hilo = jnp.concatenate([hi, lo], axis=0)
    for p in range(npair):
        ls = slice(p * LANE, (p + 1) * LANE)
        px_ref[:, ls] = _dot(l_ref[...], hilo[:, ls])
    q = q_ref[...]
    k = k_ref[...]
    lane = lax.broadcasted_iota(jnp.int32, (c, LANE), 1)
    low = lane < GLA_DK

    def put(idx, qv, kv):
        for p in range(npair):
            qp = qv[:, p * LANE:(p + 1) * LANE]
            qt_ref[idx, p, 0:c, :] = jnp.where(low, qp, 0.0).astype(BF)
            qt_ref[idx, p, c:2 * c, :] = jnp.where(low, 0.0, qp).astype(BF)
        kt_ref[idx] = kv.astype(BF)

    pblk = lambda i: px_ref[i * c:(i + 1) * c, :]
    put(0, q, k)
    put(1, q * jnp.exp(lf), k)
    for i in range(nlev - 1):
        put(2 + i, q * jnp.exp(pblk(i)), k * jnp.exp(pblk(nlev + i)))
    bcum = pblk(nlev - 1)
    put(nlev + 1, q * jnp.exp(bcum), k * jnp.exp(pblk(2 * nlev - 1)))
    decay = jnp.exp(bcum[c - 1:c, :])

    for p in range(npair):
        a = jnp.zeros((2 * c, c), F32)
        for lv in range(nlev + 1):
            mk = mask_ref[lv]
            sc = _dot_nt(qt_ref[lv, p], kt_ref[lv, :, p * LANE:(p + 1) * LANE])
            a = a + sc * jnp.concatenate([mk, mk], axis=0)
        kx = kt_ref[nlev + 1, :, p * LANE:(p + 1) * LANE]
        for hh in range(2):
            hd = 2 * p + hh
            vs = slice(hd * GLA_DV, (hd + 1) * GLA_DV)
            vh = v_ref[:, vs]
            z = z_ref[hd]
            o = (_dot(a[hh * c:(hh + 1) * c].astype(BF), vh)
                 + _dot_nt(qt_ref[nlev + 1, p, hh * c:(hh + 1) * c, :], z.astype(BF)))
            y = _rms(o, g_ref[...])
            hc_ref[:, vs] = (y * sg_ref[:, vs].astype(F32)).astype(hc_ref.dtype)
            z_ref[hd] = z * decay[:, p * LANE:(p + 1) * LANE] + _dot_tn(vh, kx)

    @pl.when(ci == nchunk - 1)
    def _():
        for hd in range(GLA_HEADS):
            zt = z_ref[hd].T
            s_ref[hd] = zt[(hd % 2) * GLA_DK:(hd % 2 + 1) * GLA_DK, :]


def _gla(gq, gk, gv, lf, sg, g_gla, l, batch, seq):
    c = GLA_CHUNK
    nchunk = seq // c
    lstack, masks = _gla_tables()
    nlev = len(_GLA_LEVELS)
    r3 = lambda a: a.reshape(batch, seq, a.shape[-1])
    blk = lambda n: pl.BlockSpec((None, c, n), lambda b, i: (b, i, 0))
    hc, s = pl.pallas_call(
        functools.partial(_gla_kernel, nchunk),
        grid=(batch, nchunk),
        in_specs=[blk(GLA_KW), blk(GLA_KW), blk(GLA_VW), blk(GLA_KW), blk(GLA_VW),
                  _const_spec(lstack.shape), _const_spec(masks.shape), _layer_spec(g_gla, l)],
        out_specs=[blk(GLA_VW), pl.BlockSpec((None, GLA_HEADS, GLA_DK, GLA_DV), lambda b, i: (b, 0, 0, 0))],
        out_shape=[jax.ShapeDtypeStruct((batch, seq, GLA_VW), BF),
                   jax.ShapeDtypeStruct((batch, GLA_HEADS, GLA_DK, GLA_DV), F32)],
        scratch_shapes=[pltpu.VMEM((2 * nlev * c, GLA_KW), F32),
                        pltpu.VMEM((nlev + 2, GLA_HEADS // 2, 2 * c, LANE), BF),
                        pltpu.VMEM((nlev + 2, c, GLA_KW), BF),
                        pltpu.VMEM((GLA_HEADS, GLA_DV, LANE), F32)],
        compiler_params=_params(("parallel", "arbitrary")),
        name="gla_prompt",
    )(r3(gq), r3(gk), r3(gv), r3(lf), r3(sg), lstack, masks, g_gla)
    return hc.reshape(batch * seq, GLA_VW), s


def _merge_kernel(x_ref, ha_ref, hb_ref, hc_ref, gmix_ref, wt_ref, wpa_ref, wpb_ref, wpc_ref, wo_ref, o_ref):
    x = x_ref[...]
    h = _rms(x, gmix_ref[...]).astype(BF)
    mix = None
    for j, (hr, wr) in enumerate(((ha_ref, wpa_ref), (hb_ref, wpb_ref), (hc_ref, wpc_ref))):
        g0 = _IN_OFF[10] + j * D_MODEL
        gate = jax.nn.sigmoid(_dot_nt(h, wt_ref[g0:g0 + D_MODEL, :]))
        term = gate * _dot(hr[...].astype(BF), wr[...])
        mix = term if mix is None else mix + term
    o_ref[...] = x + _dot(mix.astype(BF), wo_ref[...])


def _merge(x, ha, hb, hc, w, l, tm):
    r = x.shape[0]
    row = lambda n: pl.BlockSpec((tm, n), lambda i: (i, 0))
    consts = (w['g_mix'], w['wt'], w['wpa'], w['wpb'], w['wpc'], w['wo'])
    return pl.pallas_call(
        _merge_kernel,
        grid=(r // tm,),
        in_specs=[row(D_MODEL), row(ha.shape[1]), row(hb.shape[1]), row(hc.shape[1])]
                 + [_layer_spec(c, l) for c in consts],
        out_specs=row(D_MODEL),
        out_shape=jax.ShapeDtypeStruct((r, D_MODEL), F32),
        compiler_params=_params(("parallel",)),
        name="merge",
    )(x, ha, hb, hc, *consts)


def _ffn_kernel(final, x_ref, g_ref, wgu_ref, wd_ref, gf_ref, o_ref):
    x = x_ref[...]
    hn = _rms(x, g_ref[...]).astype(BF)
    acc = jnp.zeros(x.shape, F32)
    for cidx in range(D_FF // FF_CHUNK):
        g = _dot(hn, wgu_ref[0, cidx])
        u = _dot(hn, wgu_ref[1, cidx])
        act = (g * jax.nn.sigmoid(g) * u).astype(BF)
        acc = acc + _dot(act, wd_ref[cidx])
    y = x + acc
    if final:
        y = _rms(y, gf_ref[...])
    o_ref[...] = y


def _ffn(x, w, l, g_final, final, tm):
    r = x.shape[0]
    row = pl.BlockSpec((tm, D_MODEL), lambda i: (i, 0))
    consts = (w['g_ffn'], w['wgu'], w['wd'], g_final)
    return pl.pallas_call(
        functools.partial(_ffn_kernel, final),
        grid=(r // tm,),
        in_specs=[row] + [_layer_spec(c, l) for c in consts[:-1]] + [_const_spec(g_final.shape)],
        out_specs=row,
        out_shape=jax.ShapeDtypeStruct((r, D_MODEL), F32),
        compiler_params=_params(("parallel",)),
        name="ffn",
    )(x, *consts)


GLA_STEP_BATCH = 8
DECODE_PAGES = 32
DECODE_SLOTS = 3
PAGE_GROUP = 8


def _decode_kernel(layer, nch, pt_ref, ql_ref, qr_ref, cn_ref, kn_ref, ckv_hbm, krt_hbm, o_ref,
                   cbuf, rbuf, cb, rb, sem, m_ref, l_ref, acc_ref):
    npg = DECODE_PAGES
    total = ql_ref.shape[0] * nch

    def page_copies(slot, i, pid):
        rows = pl.ds(i * PAGE_SIZE, PAGE_SIZE)
        return (pltpu.make_async_copy(ckv_hbm.at[layer, pid], cbuf.at[slot, rows, :], sem.at[0, slot]),
                pltpu.make_async_copy(krt_hbm.at[layer, pid], rbuf.at[slot, :, rows], sem.at[1, slot]))

    def start_pages(step, slot, lo, hi):
        b = step // nch
        c = step % nch
        for i in range(lo, hi):
            for cp in page_copies(slot, i, pt_ref[b, c * npg + i]):
                cp.start()

    def wait(slot):
        for i in range(npg):
            for cp in page_copies(slot, i, 0):
                cp.wait()

    for s0 in range(DECODE_SLOTS - 1):
        start_pages(min(s0, total - 1), s0, 0, npg)
    ngroup = npg // PAGE_GROUP
    span = PAGE_GROUP * PAGE_SIZE
    half = KV_RANK // 2

    def body(step, carry):
        slot = step % DECODE_SLOTS
        fill = (step + DECODE_SLOTS - 1) % DECODE_SLOTS
        b = step // nch
        c = step % nch
        nxt = jnp.minimum(step + DECODE_SLOTS - 1, total - 1)
        ql = ql_ref[b]
        qr = qr_ref[b]

        @pl.when(c == 0)
        def _():
            cn = cn_ref[b]
            m_ref[...] = (jnp.sum(ql * cn, axis=-1, keepdims=True) + jnp.sum(qr * kn_ref[b], axis=-1, keepdims=True))
            l_ref[...] = jnp.ones_like(l_ref)
            acc_ref[...] = jnp.broadcast_to(cn, acc_ref.shape)

        wait(slot)
        qlb = ql.astype(BF)
        qrb = qr.astype(BF)
        ss = []
        for g in range(ngroup):
            rows = slice(g * span, (g + 1) * span)
            cb[rows, :] = cbuf[slot, rows, :].astype(BF)
            rb[:, rows] = rbuf[slot, :, rows].astype(BF)
            ss.append(_dot_nt(qlb, cb[rows, :]) + _dot(qrb, rb[:, rows]))
            start_pages(nxt, fill, g * PAGE_GROUP, (g + 1) * PAGE_GROUP)
        parts = []
        for hf in range(2):
            gs = range(hf * ngroup // 2, (hf + 1) * ngroup // 2)
            mh = ss[gs[0]].max(axis=-1, keepdims=True)
            for g in gs[1:]:
                mh = jnp.maximum(mh, ss[g].max(axis=-1, keepdims=True))
            lh = jnp.zeros_like(mh)
            ah = jnp.zeros((MLA_HEADS, KV_RANK), F32)
            for g in gs:
                p = jnp.exp2(ss[g] - mh)
                lh = lh + jnp.sum(p, axis=-1, keepdims=True)
                pb = p.astype(BF)
                rows = slice(g * span, (g + 1) * span)
                ah = ah + jnp.concatenate([_dot(pb, cb[rows, :half]), _dot(pb, cb[rows, half:])], axis=1)
            parts.append((mh, lh, ah))
        m = m_ref[...]
        m_new = jnp.maximum(jnp.maximum(m, parts[0][0]), parts[1][0])
        alpha = jnp.exp2(m - m_new)
        l = alpha * l_ref[...]
        acc = alpha * acc_ref[...]
        for mh, lh, ah in parts:
            wh = jnp.exp2(mh - m_new)
            l = l + wh * lh
            acc = acc + wh * ah
        m_ref[...] = m_new
        l_ref[...] = l
        acc_ref[...] = acc

        @pl.when(c == nch - 1)
        def _():
            o_ref[b] = acc / l

        return carry

    lax.fori_loop(0, total, body, 0)
    for extra in range(DECODE_SLOTS - 1):
        wait((total + extra) % DECODE_SLOTS)


def _decode(layer, page_table, ql, qr, ckv_new, kr_new, cache_ckv, cache_krope_t):
    b, npages = page_table.shape
    npg = DECODE_PAGES
    assert npages % npg == 0 and npg % PAGE_GROUP == 0
    keys = npg * PAGE_SIZE
    vmem = pl.BlockSpec(memory_space=pltpu.VMEM)
    hbm = pl.BlockSpec(memory_space=pl.ANY)
    return pl.pallas_call(
        functools.partial(_decode_kernel, layer, npages // npg),
        in_specs=[pl.BlockSpec(memory_space=pltpu.SMEM), vmem, vmem, vmem, vmem, hbm, hbm],
        out_specs=vmem,
        out_shape=jax.ShapeDtypeStruct((b, MLA_HEADS, KV_RANK), F32),
        scratch_shapes=[pltpu.VMEM((DECODE_SLOTS, keys, KV_RANK), F32), pltpu.VMEM((DECODE_SLOTS, ROPE_DIM, keys), F32),
                        pltpu.VMEM((keys, KV_RANK), BF), pltpu.VMEM((ROPE_DIM, keys), BF),
                        pltpu.SemaphoreType.DMA((2, DECODE_SLOTS)),
                        pltpu.VMEM((MLA_HEADS, 1), F32), pltpu.VMEM((MLA_HEADS, 1), F32),
                        pltpu.VMEM((MLA_HEADS, KV_RANK), F32)],
        compiler_params=pltpu.CompilerParams(vmem_limit_bytes=VMEM_LIMIT),
        name="mla_decode",
    )(page_table, ql, qr, ckv_new, kr_new, cache_ckv, cache_krope_t)


def _sample_ha_kernel(ol_ref, w_ref, o_ref):
    acc = None
    for hd in range(MLA_HEADS):
        t = _dot(ol_ref[hd].astype(BF), w_ref[hd])
        acc = t if acc is None else acc + t
    o_ref[...] = acc


def _sample_ha(o_lat_t, wuv_wide, l):
    b = o_lat_t.shape[1]
    return pl.pallas_call(
        _sample_ha_kernel,
        grid=(1,),
        in_specs=[_const_spec(o_lat_t.shape), _layer_spec(wuv_wide, l)],
        out_specs=pl.BlockSpec((b, MLA_HEADS * V_DIM), lambda i: (0, 0)),
        out_shape=jax.ShapeDtypeStruct((b, MLA_HEADS * V_DIM), F32),
        compiler_params=_params(("arbitrary",)),
        name="sample_ha",
    )(o_lat_t, wuv_wide)


def _gla_step_kernel(q_ref, k_ref, v_ref, lf_ref, sg_ref, g_ref, s0_ref, hc_ref, s_ref):
    dk = GLA_DK
    eye = lax.broadcasted_iota(jnp.int32, (dk, dk), 0) == lax.broadcasted_iota(jnp.int32, (dk, dk), 1)
    col = lambda rowv: jnp.sum(jnp.where(eye, jnp.broadcast_to(rowv, (dk, dk)), 0.0), axis=-1, keepdims=True)
    for bi in range(q_ref.shape[0]):
        rw = slice(bi, bi + 1)
        for hd in range(GLA_HEADS):
            ks = slice(hd * dk, (hd + 1) * dk)
            vs = slice(hd * GLA_DV, (hd + 1) * GLA_DV)
            s_new = jnp.exp(col(lf_ref[rw, ks])) * s0_ref[bi, hd] + col(k_ref[rw, ks]) * v_ref[rw, vs]
            s_ref[bi, hd] = s_new
            o = jnp.sum(col(q_ref[rw, ks]) * s_new, axis=0, keepdims=True)
            hc_ref[rw, vs] = _rms(o, g_ref[...]) * sg_ref[rw, vs]


def _gla_step(gq, gk, gv, lf, sg, g_gla, state_all, l):
    b = gq.shape[0]
    gb = GLA_STEP_BATCH
    assert b % gb == 0
    blk = lambda n: pl.BlockSpec((gb, n), lambda i: (i, 0))
    sshape = (GLA_HEADS, GLA_DK, GLA_DV)
    return pl.pallas_call(
        _gla_step_kernel,
        grid=(b // gb,),
        in_specs=[blk(GLA_KW), blk(GLA_KW), blk(GLA_VW), blk(GLA_KW), blk(GLA_VW), _layer_spec(g_gla, l),
                  pl.BlockSpec((None, gb) + sshape, lambda i: (l, i, 0, 0, 0))],
        out_specs=[blk(GLA_VW), pl.BlockSpec((gb,) + sshape, lambda i: (i, 0, 0, 0))],
        out_shape=[jax.ShapeDtypeStruct((b, GLA_VW), F32), jax.ShapeDtypeStruct((b,) + sshape, F32)],
        compiler_params=_params(("parallel",)),
        name="gla_step",
    )(gq, gk, gv, lf, sg, g_gla, state_all)


def kernel(x_prompt, x_sample, cache_ckv, cache_krope, state_gla, page_table, g_mix, w_in, g_q, w_uq, g_kv, w_uk, w_uv, g_v, b_v, w_s, b_s, w_a2, b_a, g_gla, w_pa, w_pb, w_pc, w_o, g_ffn, w_gu, w_down, g_final):
    batch, seq, _ = x_prompt.shape
    dec_b, dec_t, _ = x_sample.shape
    assert dec_t == 1 and seq % ROW_TILE == 0
    depth = w_in.shape[0]
    past = page_table.shape[1] * PAGE_SIZE
    ct_p, st_p = _rope_tables(jnp.arange(seq, dtype=F32))
    ct_s, st_s = _rope_tables(jnp.full((dec_b,), float(past), F32))
    gfin = g_final.reshape(1, -1).astype(F32)
    cache_krope_t = jnp.swapaxes(cache_krope, 2, 3)

    xp = x_prompt.reshape(batch * seq, D_MODEL)
    xs = x_sample.reshape(dec_b, D_MODEL)
    w = _prep_weights(g_mix, w_in, g_q, w_uq, g_kv, w_uk, w_uv, g_v, b_v, w_s, b_s, w_a2, b_a, g_gla,
                      w_pa, w_pb, w_pc, w_o, g_ffn, w_gu, w_down)
    ckv_p, kr_p, gla_p, ckv_s, kr_s, gla_s, gv_s = [], [], [], [], [], [], []
    for l in range(depth):
        last = l == depth - 1
        q, k, v, ckv, kr, hb, gq, gk, gv, lf, sg = _front(xp, ct_p, st_p, w, l, True, seq)
        ha = _attention(q, k, v, batch, seq)
        hc, s_c = _gla(gq, gk, gv, lf, sg, w['g_gla'], l, batch, seq)
        xp = _ffn(_merge(xp, ha, hb, hc, w, l, ROW_TILE), w, l, gfin, last, ROW_TILE)
        ckv_p.append(ckv.reshape(batch, seq, KV_RANK))
        kr_p.append(kr)
        gla_p.append(s_c)
        q, ql, ckv, kr, hb, vn, gq, gk, gv, lf, sg = _front(xs, ct_s, st_s, w, l, False, seq)
        qr = q.reshape(dec_b, MLA_HEADS, HEAD_PAD)[:, :, NOPE_DIM:NOPE_DIM + ROPE_DIM]
        o_lat = _decode(l, page_table, ql.reshape(dec_b, MLA_HEADS, KV_RANK), qr,
                        ckv.reshape(dec_b, 1, KV_RANK), kr.reshape(dec_b, 1, ROPE_DIM), cache_ckv, cache_krope_t)
        ha = _sample_ha(o_lat.transpose(1, 0, 2), w['wuv_wide'], l)
        hc, s_c = _gla_step(gq, gk, gv, lf, sg, w['g_gla'], state_gla, l)
        xs = _ffn(_merge(xs, ha, hb, hc, w, l, dec_b), w, l, gfin, last, dec_b)
        ckv_s.append(ckv.reshape(dec_b, 1, KV_RANK))
        kr_s.append(kr.reshape(dec_b, 1, ROPE_DIM))
        gla_s.append(s_c)
        gv_s.append(vn.reshape(dec_b, 1, GMLP_WIDTH))
    return (xp.reshape(batch, seq, D_MODEL), xs.reshape(dec_b, 1, D_MODEL), jnp.stack(ckv_p), jnp.swapaxes(jnp.stack(kr_p), 2, 3),
            jnp.stack(gla_p), jnp.stack(ckv_s), jnp.stack(kr_s), jnp.stack(gla_s), jnp.stack(gv_s))
```

```python
import functools
import math

import numpy as np
import jax
import jax.numpy as jnp
from jax import lax
from jax.experimental import pallas as pl
from jax.experimental.pallas import tpu as pltpu

D_MODEL = 1024
PAGE_SIZE = 128
MLA_HEADS = 8
Q_RANK = 384
KV_RANK = 256
NOPE_DIM = 64
ROPE_DIM = 32
V_DIM = 64
ROPE_THETA = 10000.0
GMLP_GROUPS = 8
GMLP_CHUNK = 128
GMLP_WIDTH = 512
GLA_HEADS = 4
GLA_DK = 64
GLA_DV = 128
GLA_KW = GLA_HEADS * GLA_DK
GLA_VW = GLA_HEADS * GLA_DV
GLA_GATE_RANK = 16
GLA_TAU = 16.0
N_BRANCH = 3
D_FF = 2816
IN_SPLITS = (Q_RANK, KV_RANK, ROPE_DIM, GMLP_WIDTH, GMLP_WIDTH, GLA_KW, GLA_KW, GLA_VW,
             GLA_GATE_RANK, GLA_VW, N_BRANCH * D_MODEL)
_IN_OFF = tuple(int(v) for v in np.cumsum((0,) + IN_SPLITS))

LANE = 128
SUB = 8
HEAD_PAD = 128
GLA_CHUNK = 128
ROW_TILE = 512
ATTN_TILE = 256
ATTN_Q_TILE = 512
ATTN_HEAD_BATCH = 8
ATTN_ONES_ROWS = 16
FF_CHUNK = 256
VMEM_LIMIT = 56 * 1024 * 1024

BF = jnp.bfloat16
F32 = jnp.float32


def _dot(a, b):
    return jnp.dot(a, b, preferred_element_type=F32)


def _dot_nt(a, b):
    return lax.dot_general(a, b, (((1,), (1,)), ((), ())), preferred_element_type=F32)


def _dot_tn(a, b):
    return lax.dot_general(a, b, (((0,), (0,)), ((), ())), preferred_element_type=F32)


def _rms(x, g, eps=1e-6):
    return x * lax.rsqrt(jnp.mean(x * x, axis=-1, keepdims=True) + eps) * g


def _const_spec(shape):
    nd = len(shape)
    return pl.BlockSpec(shape, lambda *_: (0,) * nd, pipeline_mode=pl.Buffered(1))


def _layer_spec(arr, l):
    nd = arr.ndim - 1
    return pl.BlockSpec((None,) + arr.shape[1:], lambda *_: (l,) + (0,) * nd, pipeline_mode=pl.Buffered(1))


def _params(sem):
    return pltpu.CompilerParams(dimension_semantics=sem, vmem_limit_bytes=VMEM_LIMIT)


def _prep_weights(g_mix, w_in, g_q, w_uq, g_kv, w_uk, w_uv, g_v, b_v, w_s, b_s, w_a2, b_a, g_gla,
                  w_pa, w_pb, w_pc, w_o, g_ffn, w_gu, w_down):
    d = w_in.shape[0]
    half = ROPE_DIM // 2
    wt = jnp.swapaxes(w_in, 1, 2).astype(BF)
    kr = wt[:, _IN_OFF[2]:_IN_OFF[3]]
    kr_swap = jnp.concatenate([-kr[:, half:], kr[:, :half]], axis=1)
    z16 = jnp.zeros((d, 16, D_MODEL), BF)
    z32 = jnp.zeros((d, 32, D_MODEL), BF)
    small_t = jnp.concatenate([kr_swap, wt[:, _IN_OFF[8]:_IN_OFF[9]], z16, kr, z32], axis=1)

    qn, qr = w_uq[..., :NOPE_DIM], w_uq[..., NOPE_DIM:]
    qr_swap = jnp.concatenate([-qr[..., half:], qr[..., :half]], axis=-1)
    wq_cat = jnp.concatenate([qn, qr, qr_swap], axis=-1).reshape(d, Q_RANK, MLA_HEADS * HEAD_PAD).astype(BF)

    wk_pad = jnp.concatenate([w_uk, jnp.zeros_like(w_uk)], axis=-1).reshape(d, KV_RANK, MLA_HEADS * HEAD_PAD).astype(BF)
    wuvt = w_uv.reshape(d, KV_RANK, MLA_HEADS * V_DIM).transpose(0, 2, 1).astype(BF)
    wukt = jnp.concatenate([w_uk.transpose(0, 2, 3, 1),
                            jnp.zeros((d, MLA_HEADS, HEAD_PAD - NOPE_DIM, KV_RANK), F32)], axis=2).astype(BF)
    eye_h = jnp.eye(MLA_HEADS, dtype=F32)
    wuv_wide = jnp.einsum('lrhd,hg->lhrgd', w_uv, eye_h).reshape(d, MLA_HEADS, KV_RANK, MLA_HEADS * V_DIM).astype(BF)

    wa2_pad = jnp.zeros((d, LANE, GLA_KW), F32).at[:, 32:32 + GLA_GATE_RANK].set(w_a2).astype(BF)
    tri = jnp.tril(jnp.ones((GMLP_CHUNK, GMLP_CHUNK), bool))
    ws = jnp.where(tri[None, None], w_s, 0.0).astype(BF).reshape(d, GMLP_GROUPS // 2, 2 * GMLP_CHUNK, GMLP_CHUNK)
    gd = GMLP_WIDTH // GMLP_GROUPS
    bias_tab = jnp.repeat(b_s.transpose(0, 2, 1), gd, axis=2)
    ws00 = jnp.repeat(w_s[:, :, 0, 0], gd, axis=1)[:, None, :]
    bs0 = jnp.repeat(b_s[:, :, 0], gd, axis=1)[:, None, :]

    nff = D_FF // FF_CHUNK
    wgu = jnp.swapaxes(jnp.swapaxes(w_gu, 1, 2).reshape(d, 2, nff, FF_CHUNK, D_MODEL), 3, 4).astype(BF)
    wd = w_down.reshape(d, nff, FF_CHUNK, D_MODEL).astype(BF)
    row = lambda a: a.reshape(d, 1, -1).astype(F32)
    return dict(
        wt=wt, small_t=small_t, wgu=wgu, wq_cat=wq_cat, wk_pad=wk_pad, wuvt=wuvt, wukt=wukt,
        wuv_wide=wuv_wide, wa2_pad=wa2_pad, ws=ws, bias_tab=bias_tab, ws00=ws00, bs0=bs0,
        g_mix=row(g_mix), g_q=row(g_q), g_kv=row(g_kv), g_v=row(g_v), b_v=row(b_v),
        b_a=row(b_a), g_gla=row(g_gla), g_ffn=row(g_ffn),
        wpa=w_pa.astype(BF), wpb=w_pb.astype(BF), wpc=w_pc.astype(BF), wo=w_o.astype(BF),
        wd=wd)


def _rope_tables(pos):
    half = ROPE_DIM // 2
    inv = jnp.power(ROPE_THETA, -jnp.arange(half, dtype=F32) * 2.0 / ROPE_DIM)
    ang = pos[:, None] * inv[None, :]
    cos, sin = jnp.cos(ang), jnp.sin(ang)
    t = pos.shape[0]
    ctab = jnp.concatenate([jnp.ones((t, NOPE_DIM), F32), cos, cos, jnp.zeros((t, 32), F32)], axis=1)
    stab = jnp.concatenate([jnp.zeros((t, NOPE_DIM), F32), sin, sin, jnp.zeros((t, 32), F32)], axis=1)
    return ctab, stab


def _front_kernel(prompt, tm, x_ref, ct_ref, st_ref, wt_ref, small_ref, wq_cat_ref, wkv_a_ref, wkv_b_ref,
                  wa2_ref, gm_a_ref, gm_b_ref, gmix_ref, gq_ref, gkv_ref, gv_ref, bv_ref, ba_ref, *outs):
    if prompt:
        (q_ref, k_ref, v_ref, ckv_ref, kr_ref, hb_ref, oq_ref, ok_ref, ov_ref, lf_ref, sg_ref) = outs
    else:
        (q_ref, ql_ref, ckv_ref, kr_ref, hb_ref, vn_ref, oq_ref, ok_ref, ov_ref, lf_ref, sg_ref) = outs
    x = x_ref[...]
    h = _rms(x, gmix_ref[...]).astype(BF)
    proj = lambda i, j: _dot_nt(h, wt_ref[_IN_OFF[i]:_IN_OFF[j], :])
    z_qkv = proj(0, 2)
    z_uv = proj(3, 5)
    z_gla = proj(5, 8)
    ct = ct_ref[...]
    st = st_ref[...]
    scale = (NOPE_DIM + ROPE_DIM) ** -0.5 * math.log2(math.e)

    cq = _rms(z_qkv[:, :Q_RANK], gq_ref[...]).astype(BF)
    qa = _dot(cq, wq_cat_ref[...])
    for hd in range(MLA_HEADS):
        sl = slice(hd * HEAD_PAD, (hd + 1) * HEAD_PAD)
        qh = (qa[:, sl] * ct + pltpu.roll(qa[:, sl], HEAD_PAD - ROPE_DIM, 1) * st) * scale
        q_ref[:, sl] = qh.astype(q_ref.dtype)
        if not prompt:
            ql_ref[:, hd * KV_RANK:(hd + 1) * KV_RANK] = _dot(qh.astype(BF), wkv_a_ref[hd])

    ckv = _rms(z_qkv[:, Q_RANK:], gkv_ref[...])
    ckv_ref[...] = ckv
    small = _dot_nt(h, small_ref[...])
    lane = lax.broadcasted_iota(jnp.int32, small.shape, 1)
    kr_rot = jnp.where((lane >= NOPE_DIM) & (lane < NOPE_DIM + ROPE_DIM),
                       small * ct + pltpu.roll(small, NOPE_DIM, 1) * st, 0.0)
    if prompt:
        kr_ref[...] = kr_rot.T[NOPE_DIM:NOPE_DIM + ROPE_DIM, :]
    else:
        kr_ref[...] = pltpu.roll(kr_rot, NOPE_DIM, 1)[:, :ROPE_DIM]
    if prompt:
        ckvb = ckv.astype(BF)
        kn = _dot(ckvb, wkv_a_ref[...])
        for hd in range(MLA_HEADS):
            sl = slice(hd * HEAD_PAD, (hd + 1) * HEAD_PAD)
            k_ref[:, sl] = (kn[:, sl] + kr_rot).astype(BF)
        vt = _dot_nt(wkv_b_ref[...], ckvb).astype(BF)
        ones = jnp.ones((V_DIM, ATTN_TILE), BF)
        for t in range(tm // ATTN_TILE):
            ts = slice(t * ATTN_TILE, (t + 1) * ATTN_TILE)
            for hd in range(MLA_HEADS):
                val = hd * HEAD_PAD + (hd % 2) * V_DIM
                pad = hd * HEAD_PAD + (1 - hd % 2) * V_DIM
                v_ref[t, val:val + V_DIM, :] = vt[hd * V_DIM:(hd + 1) * V_DIM, ts]
                v_ref[t, pad:pad + V_DIM, :] = ones

    u = z_uv[:, :GMLP_WIDTH]
    v = z_uv[:, GMLP_WIDTH:]
    mu = jnp.mean(v, axis=-1, keepdims=True)
    vc = v - mu
    var = jnp.mean(vc * vc, axis=-1, keepdims=True)
    vn = vc * lax.rsqrt(var + 1e-5) * gv_ref[...] + bv_ref[...]
    if prompt:
        vnb = vn.astype(BF)
        lane2 = lax.broadcasted_iota(jnp.int32, (GMLP_CHUNK, LANE), 1)
        gd = GMLP_WIDTH // GMLP_GROUPS
        for c in range(0, tm // GMLP_CHUNK, 2):
            ra = slice(c * GMLP_CHUNK, (c + 1) * GMLP_CHUNK)
            rb = slice((c + 1) * GMLP_CHUNK, (c + 2) * GMLP_CHUNK)
            for gp in range(GMLP_GROUPS // 2):
                cs = slice(gp * LANE, (gp + 1) * LANE)
                mm = _dot(gm_a_ref[gp], jnp.concatenate([vnb[ra, cs], vnb[rb, cs]], axis=1))
                for rs, ls in ((ra, slice(0, LANE)), (rb, slice(LANE, 2 * LANE))):
                    mixed = jnp.where(lane2 < gd, mm[:GMLP_CHUNK, ls], mm[GMLP_CHUNK:, ls])
                    hb_ref[rs, cs] = (u[rs, cs] * (mixed + gm_b_ref[:, cs])).astype(hb_ref.dtype)
    else:
        vn_ref[...] = vn
        hb_ref[...] = (u * (gm_a_ref[...] * vn + gm_b_ref[...])).astype(hb_ref.dtype)

    oq_ref[...] = z_gla[:, :GLA_KW] * (GLA_DK ** -0.5)
    ok_ref[...] = z_gla[:, GLA_KW:2 * GLA_KW]
    ov_ref[...] = z_gla[:, 2 * GLA_KW:].astype(ov_ref.dtype)
    a = _dot(small.astype(BF), wa2_ref[...]) + ba_ref[...]
    lf_ref[...] = (jnp.minimum(a, 0.0) - jnp.log(1.0 + jnp.exp(-jnp.abs(a)))) * (1.0 / GLA_TAU)
    gg = proj(9, 10)
    sg_ref[...] = (gg * jax.nn.sigmoid(gg)).astype(sg_ref.dtype)


def _front(x, ctab, stab, w, l, prompt, seq):
    r = x.shape[0]
    tm = ROW_TILE if prompt else r
    nt = r // tm
    row = lambda n: pl.BlockSpec((tm, n), lambda i: (i, 0))
    rows = lambda n, dt: (row(n), jax.ShapeDtypeStruct((r, n), dt))
    width = MLA_HEADS * HEAD_PAD
    if prompt:
        tab = pl.BlockSpec((tm, LANE), lambda i: (i % (seq // tm), 0))
        wkv_a, wkv_b = w['wk_pad'], w['wuvt']
        gm_a, gm_b = w['ws'], w['bias_tab']
        vt_out = (pl.BlockSpec((tm // ATTN_TILE, width, ATTN_TILE), lambda i: (i, 0, 0)),
                  jax.ShapeDtypeStruct((r // ATTN_TILE, width, ATTN_TILE), BF))
        nst = seq // tm
        krt_out = (pl.BlockSpec((None, ROPE_DIM, tm), lambda i: (i // nst, 0, i % nst)),
                   jax.ShapeDtypeStruct((r // seq, ROPE_DIM, seq), F32))
        outs = (rows(width, BF), rows(width, BF), vt_out, rows(KV_RANK, F32), krt_out,
                rows(GMLP_WIDTH, BF), rows(GLA_KW, F32), rows(GLA_KW, F32), rows(GLA_VW, BF), rows(GLA_KW, F32),
                rows(GLA_VW, BF))
    else:
        tab = pl.BlockSpec((tm, LANE), lambda i: (0, 0))
        wkv_a, wkv_b = w['wukt'], w['wuvt']
        gm_a, gm_b = w['ws00'], w['bs0']
        outs = (rows(width, F32), rows(MLA_HEADS * KV_RANK, F32), rows(KV_RANK, F32), rows(ROPE_DIM, F32),
                rows(GMLP_WIDTH, BF), rows(GMLP_WIDTH, F32), rows(GLA_KW, F32), rows(GLA_KW, F32), rows(GLA_VW, F32),
                rows(GLA_KW, F32), rows(GLA_VW, F32))
    consts = (w['wt'], w['small_t'], w['wq_cat'], wkv_a, wkv_b, w['wa2_pad'], gm_a, gm_b,
              w['g_mix'], w['g_q'], w['g_kv'], w['g_v'], w['b_v'], w['b_a'])
    return pl.pallas_call(
        functools.partial(_front_kernel, prompt, tm),
        grid=(nt,),
        in_specs=[row(D_MODEL), tab, tab] + [_layer_spec(c, l) for c in consts],
        out_specs=[o[0] for o in outs],
        out_shape=[o[1] for o in outs],
        compiler_params=_params(("parallel",)),
        name="front_prompt" if prompt else "front_sample",
    )(x, ctab, stab, *consts)


def _attn_kernel(q_ref, k_ref, vt_ref, o_ref, m_ref, acc_ref, st_ref, p_ref):
    tk, tq = ATTN_TILE, ATTN_Q_TILE
    ratio = tq // tk
    qi = pl.program_id(1)
    m_ref[...] = jnp.full(m_ref.shape, -jnp.inf, F32)
    acc_ref[...] = jnp.zeros(acc_ref.shape, F32)

    def tile(j, diag):
        off = pl.multiple_of(j * tk, tk)
        qs = slice(0 if diag is None else diag * tk, tq)
        for h0 in range(0, MLA_HEADS, ATTN_HEAD_BATCH):
            heads = [(hd, slice(hd * HEAD_PAD, (hd + 1) * HEAD_PAD)) for hd in range(h0, h0 + ATTN_HEAD_BATCH)]
            mx = {}
            for hd, cs in heads:
                st = _dot_nt(k_ref[pl.ds(off, tk), cs], q_ref[qs, cs])
                if diag is not None:
                    w = tq - diag * tk
                    visible = (lax.broadcasted_iota(jnp.int32, (tk, w), 0)
                               <= lax.broadcasted_iota(jnp.int32, (tk, w), 1))
                    st = jnp.where(visible, st, -jnp.inf)
                st_ref[hd, :, qs] = st
                mx[hd] = jnp.broadcast_to(jnp.max(st, axis=0, keepdims=True), (SUB, st.shape[1]))
            alpha = {}
            for hd, cs in heads:
                m = m_ref[hd, :, qs]
                m_new = jnp.maximum(m, mx[hd])
                alpha[hd] = jnp.exp2(m - m_new)
                m_ref[hd, :, qs] = m_new
                w = m_new.shape[1]
                e = jnp.exp2(st_ref[hd, :, qs].reshape(tk // SUB, SUB, w) - m_new[None])
                p_ref[hd, :, qs] = e.reshape(tk, w).astype(BF)
            for hd, cs in heads:
                nrow = V_DIM + ATTN_ONES_ROWS
                r0 = (hd % 2) * (V_DIM - ATTN_ONES_ROWS)
                rs = slice(r0, r0 + nrow)
                vrows = slice(hd * HEAD_PAD + r0, hd * HEAD_PAD + r0 + nrow)
                pv = _dot(vt_ref[j, vrows, :], p_ref[hd, :, qs])
                w = pv.shape[1]
                old = acc_ref[hd, rs, qs].reshape(nrow // SUB, SUB, w) * alpha[hd][None]
                acc_ref[hd, rs, qs] = old.reshape(nrow, w) + pv

    def body(j, carry):
        tile(j, None)
        return carry

    lax.fori_loop(0, qi * ratio, body, 0)
    for d in range(ratio):
        tile(qi * ratio + d, d)
    rowi = lax.broadcasted_iota(jnp.int32, (HEAD_PAD, tq), 0)
    for pr in range(MLA_HEADS // 2):
        ae = acc_ref[2 * pr]
        ao = acc_ref[2 * pr + 1]
        out_t = jnp.where(rowi < V_DIM, ae / ae[V_DIM:V_DIM + 1, :], ao / ao[V_DIM - 1:V_DIM, :])
        o_ref[:, pr * HEAD_PAD:(pr + 1) * HEAD_PAD] = out_t.T.astype(o_ref.dtype)


def _attention(q, k, vt, batch, seq):
    tk, tq = ATTN_TILE, ATTN_Q_TILE
    width = MLA_HEADS * HEAD_PAD
    q3, k3 = (a.reshape(batch, seq, width) for a in (q, k))
    vt4 = vt.reshape(batch, seq // tk, width, tk)
    out = pl.pallas_call(
        _attn_kernel,
        grid=(batch, seq // tq),
        in_specs=[pl.BlockSpec((None, tq, width), lambda b, i: (b, i, 0)),
                  pl.BlockSpec((None, seq, width), lambda b, i: (b, 0, 0)),
                  pl.BlockSpec((None, seq // tk, width, tk), lambda b, i: (b, 0, 0, 0))],
        out_specs=pl.BlockSpec((None, tq, MLA_HEADS * V_DIM), lambda b, i: (b, i, 0)),
        out_shape=jax.ShapeDtypeStruct((batch, seq, MLA_HEADS * V_DIM), BF),
        scratch_shapes=[pltpu.VMEM((MLA_HEADS, SUB, tq), F32), pltpu.VMEM((MLA_HEADS, HEAD_PAD, tq), F32),
                        pltpu.VMEM((MLA_HEADS, tk, tq), F32), pltpu.VMEM((MLA_HEADS, tk, tq), BF)],
        compiler_params=_params(("parallel", "arbitrary")),
        name="mla_prompt_attn",
    )(q3, k3, vt4)
    return out.reshape(batch * seq, MLA_HEADS * V_DIM)


_GLA_LEVELS = (1, 2, 4, 8, 16, 32, 64)


def _gla_tables():
    c = GLA_CHUNK
    t = np.arange(c)[:, None]
    r = np.arange(c)[None, :]
    blocks = []
    for m in _GLA_LEVELS[1:] + (c,):
        blocks.append(((t // m == r // m) & (r <= t)))
    for m in _GLA_LEVELS[1:] + (c,):
        blocks.append(((t // m == r // m) & (r > t)))
    lstack = np.concatenate(blocks, axis=0).astype(np.float32)
    lstack = np.concatenate([lstack, lstack], axis=1)
    masks = [np.eye(c, dtype=np.float32)]
    for m in _GLA_LEVELS:
        masks.append((((t // m) % 2 == 1) & (r // m == t // m - 1)).astype(np.float32))
    return jnp.asarray(lstack, BF), jnp.asarray(np.stack(masks), F32)


def _gla_kernel(nchunk, q_ref, k_ref, v_ref, lf_ref, sg_ref, l_ref, mask_ref, g_ref, hc_ref, s_ref,
                px_ref, qt_ref, kt_ref, z_ref):
    c = GLA_CHUNK
    ci = pl.program_id(1)
    nlev = len(_GLA_LEVELS)
    npair = GLA_HEADS // 2

    @pl.when(ci == 0)
    def _():
        z_ref[...] = jnp.zeros_like(z_ref)

    lf = lf_ref[...]
    hi = lf.astype(BF)
    lo = (lf - hi.astype(F32)).astype(BF)
    hilo = jnp.concatenate([hi, lo], axis=0)
    for p in range(npair):
        ls = slice(p * LANE, (p + 1) * LANE)
        px_ref[:, ls] = _dot(l_ref[...], hilo[:, ls])
    q = q_ref[...]
    k = k_ref[...]
    lane = lax.broadcasted_iota(jnp.int32, (c, LANE), 1)
    low = lane < GLA_DK

    def put(idx, qv, kv):
        for p in range(npair):
            qp = qv[:, p * LANE:(p + 1) * LANE]
            qt_ref[idx, p, 0:c, :] = jnp.where(low, qp, 0.0).astype(BF)
            qt_ref[idx, p, c:2 * c, :] = jnp.where(low, 0.0, qp).astype(BF)
        kt_ref[idx] = kv.astype(BF)

    pblk = lambda i: px_ref[i * c:(i + 1) * c, :]
    put(0, q, k)
    put(1, q * jnp.exp(lf), k)
    for i in range(nlev - 1):
        put(2 + i, q * jnp.exp(pblk(i)), k * jnp.exp(pblk(nlev + i)))
    bcum = pblk(nlev - 1)
    put(nlev + 1, q * jnp.exp(bcum), k * jnp.exp(pblk(2 * nlev - 1)))
    decay = jnp.exp(bcum[c - 1:c, :])

    for p in range(npair):
        a = jnp.zeros((2 * c, c), F32)
        for lv in range(nlev + 1):
            mk = mask_ref[lv]
            sc = _dot_nt(qt_ref[lv, p], kt_ref[lv, :, p * LANE:(p + 1) * LANE])
            a = a + sc * jnp.concatenate([mk, mk], axis=0)
        kx = kt_ref[nlev + 1, :, p * LANE:(p + 1) * LANE]
        for hh in range(2):
            hd = 2 * p + hh
            vs = slice(hd * GLA_DV, (hd + 1) * GLA_DV)
            vh = v_ref[:, vs]
            z = z_ref[hd]
            o = (_dot(a[hh * c:(hh + 1) * c].astype(BF), vh)
                 + _dot_nt(qt_ref[nlev + 1, p, hh * c:(hh + 1) * c, :], z.astype(BF)))
            y = _rms(o, g_ref[...])
            hc_ref[:, vs] = (y * sg_ref[:, vs].astype(F32)).astype(hc_ref.dtype)
            z_ref[hd] = z * decay[:, p * LANE:(p + 1) * LANE] + _dot_tn(vh, kx)

    @pl.when(ci == nchunk - 1)
    def _():
        for hd in range(GLA_HEADS):
            zt = z_ref[hd].T
            s_ref[hd] = zt[(hd % 2) * GLA_DK:(hd % 2 + 1) * GLA_DK, :]


def _gla(gq, gk, gv, lf, sg, g_gla, l, batch, seq):
    c = GLA_CHUNK
    nchunk = seq // c
    lstack, masks = _gla_tables()
    nlev = len(_GLA_LEVELS)
    r3 = lambda a: a.reshape(batch, seq, a.shape[-1])
    blk = lambda n: pl.BlockSpec((None, c, n), lambda b, i: (b, i, 0))
    hc, s = pl.pallas_call(
        functools.partial(_gla_kernel, nchunk),
        grid=(batch, nchunk),
        in_specs=[blk(GLA_KW), blk(GLA_KW), blk(GLA_VW), blk(GLA_KW), blk(GLA_VW),
                  _const_spec(lstack.shape), _const_spec(masks.shape), _layer_spec(g_gla, l)],
        out_specs=[blk(GLA_VW), pl.BlockSpec((None, GLA_HEADS, GLA_DK, GLA_DV), lambda b, i: (b, 0, 0, 0))],
        out_shape=[jax.ShapeDtypeStruct((batch, seq, GLA_VW), BF),
                   jax.ShapeDtypeStruct((batch, GLA_HEADS, GLA_DK, GLA_DV), F32)],
        scratch_shapes=[pltpu.VMEM((2 * nlev * c, GLA_KW), F32),
                        pltpu.VMEM((nlev + 2, GLA_HEADS // 2, 2 * c, LANE), BF),
                        pltpu.VMEM((nlev + 2, c, GLA_KW), BF),
                        pltpu.VMEM((GLA_HEADS, GLA_DV, LANE), F32)],
        compiler_params=_params(("parallel", "arbitrary")),
        name="gla_prompt",
    )(r3(gq), r3(gk), r3(gv), r3(lf), r3(sg), lstack, masks, g_gla)
    return hc.reshape(batch * seq, GLA_VW), s


def _merge_kernel(x_ref, ha_ref, hb_ref, hc_ref, gmix_ref, wt_ref, wpa_ref, wpb_ref, wpc_ref, wo_ref, o_ref):
    x = x_ref[...]
    h = _rms(x, gmix_ref[...]).astype(BF)
    mix = None
    for j, (hr, wr) in enumerate(((ha_ref, wpa_ref), (hb_ref, wpb_ref), (hc_ref, wpc_ref))):
        g0 = _IN_OFF[10] + j * D_MODEL
        gate = jax.nn.sigmoid(_dot_nt(h, wt_ref[g0:g0 + D_MODEL, :]))
        term = gate * _dot(hr[...].astype(BF), wr[...])
        mix = term if mix is None else mix + term
    o_ref[...] = x + _dot(mix.astype(BF), wo_ref[...])


def _merge(x, ha, hb, hc, w, l, tm):
    r = x.shape[0]
    row = lambda n: pl.BlockSpec((tm, n), lambda i: (i, 0))
    consts = (w['g_mix'], w['wt'], w['wpa'], w['wpb'], w['wpc'], w['wo'])
    return pl.pallas_call(
        _merge_kernel,
        grid=(r // tm,),
        in_specs=[row(D_MODEL), row(ha.shape[1]), row(hb.shape[1]), row(hc.shape[1])]
                 + [_layer_spec(c, l) for c in consts],
        out_specs=row(D_MODEL),
        out_shape=jax.ShapeDtypeStruct((r, D_MODEL), F32),
        compiler_params=_params(("parallel",)),
        name="merge",
    )(x, ha, hb, hc, *consts)


def _ffn_kernel(final, x_ref, g_ref, wgu_ref, wd_ref, gf_ref, o_ref):
    x = x_ref[...]
    hn = _rms(x, g_ref[...]).astype(BF)
    acc = jnp.zeros(x.shape, F32)
    for cidx in range(D_FF // FF_CHUNK):
        g = _dot(hn, wgu_ref[0, cidx])
        u = _dot(hn, wgu_ref[1, cidx])
        act = (g * jax.nn.sigmoid(g) * u).astype(BF)
        acc = acc + _dot(act, wd_ref[cidx])
    y = x + acc
    if final:
        y = _rms(y, gf_ref[...])
    o_ref[...] = y


def _ffn(x, w, l, g_final, final, tm):
    r = x.shape[0]
    row = pl.BlockSpec((tm, D_MODEL), lambda i: (i, 0))
    consts = (w['g_ffn'], w['wgu'], w['wd'], g_final)
    return pl.pallas_call(
        functools.partial(_ffn_kernel, final),
        grid=(r // tm,),
        in_specs=[row] + [_layer_spec(c, l) for c in consts[:-1]] + [_const_spec(g_final.shape)],
        out_specs=row,
        out_shape=jax.ShapeDtypeStruct((r, D_MODEL), F32),
        compiler_params=_params(("parallel",)),
        name="ffn",
    )(x, *consts)


GLA_STEP_BATCH = 8
DECODE_PAGES = 32
DECODE_SLOTS = 3
PAGE_GROUP = 8


def _decode_kernel(layer, nch, pt_ref, ql_ref, qr_ref, cn_ref, kn_ref, ckv_hbm, krt_hbm, o_ref,
                   cbuf, rbuf, cb, rb, sem, m_ref, l_ref, acc_ref):
    npg = DECODE_PAGES
    total = ql_ref.shape[0] * nch

    def page_copies(slot, i, pid):
        rows = pl.ds(i * PAGE_SIZE, PAGE_SIZE)
        return (pltpu.make_async_copy(ckv_hbm.at[layer, pid], cbuf.at[slot, rows, :], sem.at[0, slot]),
                pltpu.make_async_copy(krt_hbm.at[layer, pid], rbuf.at[slot, :, rows], sem.at[1, slot]))

    def start_pages(step, slot, lo, hi):
        b = step // nch
        c = step % nch
        for i in range(lo, hi):
            for cp in page_copies(slot, i, pt_ref[b, c * npg + i]):
                cp.start(priority=i % 2)

    def wait(slot):
        for i in range(npg):
            for cp in page_copies(slot, i, 0):
                cp.wait()

    for s0 in range(DECODE_SLOTS - 1):
        start_pages(min(s0, total - 1), s0, 0, npg)
    ngroup = npg // PAGE_GROUP
    span = PAGE_GROUP * PAGE_SIZE
    half = KV_RANK // 2

    def body(step, carry):
        slot = step % DECODE_SLOTS
        fill = (step + DECODE_SLOTS - 1) % DECODE_SLOTS
        b = step // nch
        c = step % nch
        nxt = jnp.minimum(step + DECODE_SLOTS - 1, total - 1)
        ql = ql_ref[b]
        qr = qr_ref[b]

        @pl.when(c == 0)
        def _():
            cn = cn_ref[b]
            m_ref[...] = (jnp.sum(ql * cn, axis=-1, keepdims=True) + jnp.sum(qr * kn_ref[b], axis=-1, keepdims=True))
            l_ref[...] = jnp.ones_like(l_ref)
            acc_ref[...] = jnp.broadcast_to(cn, acc_ref.shape)

        wait(slot)
        qlb = ql.astype(BF)
        qrb = qr.astype(BF)
        ss = []
        for g in range(ngroup):
            rows = slice(g * span, (g + 1) * span)
            cb[rows, :] = cbuf[slot, rows, :].astype(BF)
            rb[:, rows] = rbuf[slot, :, rows].astype(BF)
            ss.append(_dot_nt(qlb, cb[rows, :]) + _dot(qrb, rb[:, rows]))
            start_pages(nxt, fill, g * PAGE_GROUP, (g + 1) * PAGE_GROUP)
        parts = []
        for hf in range(2):
            gs = range(hf * ngroup // 2, (hf + 1) * ngroup // 2)
            mh = ss[gs[0]].max(axis=-1, keepdims=True)
            for g in gs[1:]:
                mh = jnp.maximum(mh, ss[g].max(axis=-1, keepdims=True))
            lh = jnp.zeros_like(mh)
            ah = jnp.zeros((MLA_HEADS, KV_RANK), F32)
            for g in gs:
                p = jnp.exp2(ss[g] - mh)
                lh = lh + jnp.sum(p, axis=-1, keepdims=True)
                pb = p.astype(BF)
                rows = slice(g * span, (g + 1) * span)
                ah = ah + jnp.concatenate([_dot(pb, cb[rows, :half]), _dot(pb, cb[rows, half:])], axis=1)
            parts.append((mh, lh, ah))
        m = m_ref[...]
        m_new = jnp.maximum(jnp.maximum(m, parts[0][0]), parts[1][0])
        alpha = jnp.exp2(m - m_new)
        l = alpha * l_ref[...]
        acc = alpha * acc_ref[...]
        for mh, lh, ah in parts:
            wh = jnp.exp2(mh - m_new)
            l = l + wh * lh
            acc = acc + wh * ah
        m_ref[...] = m_new
        l_ref[...] = l
        acc_ref[...] = acc

        @pl.when(c == nch - 1)
        def _():
            o_ref[b] = acc / l

        return carry

    lax.fori_loop(0, total, body, 0)
    for extra in range(DECODE_SLOTS - 1):
        wait((total + extra) % DECODE_SLOTS)


def _decode(layer, page_table, ql, qr, ckv_new, kr_new, cache_ckv, cache_krope_t):
    b, npages = page_table.shape
    npg = DECODE_PAGES
    assert npages % npg == 0 and npg % PAGE_GROUP == 0
    keys = npg * PAGE_SIZE
    vmem = pl.BlockSpec(memory_space=pltpu.VMEM)
    hbm = pl.BlockSpec(memory_space=pl.ANY)
    return pl.pallas_call(
        functools.partial(_decode_kernel, layer, npages // npg),
        in_specs=[pl.BlockSpec(memory_space=pltpu.SMEM), vmem, vmem, vmem, vmem, hbm, hbm],
        out_specs=vmem,
        out_shape=jax.ShapeDtypeStruct((b, MLA_HEADS, KV_RANK), F32),
        scratch_shapes=[pltpu.VMEM((DECODE_SLOTS, keys, KV_RANK), F32), pltpu.VMEM((DECODE_SLOTS, ROPE_DIM, keys), F32),
                        pltpu.VMEM((keys, KV_RANK), BF), pltpu.VMEM((ROPE_DIM, keys), BF),
                        pltpu.SemaphoreType.DMA((2, DECODE_SLOTS)),
                        pltpu.VMEM((MLA_HEADS, 1), F32), pltpu.VMEM((MLA_HEADS, 1), F32),
                        pltpu.VMEM((MLA_HEADS, KV_RANK), F32)],
        compiler_params=pltpu.CompilerParams(vmem_limit_bytes=VMEM_LIMIT),
        name="mla_decode",
    )(page_table, ql, qr, ckv_new, kr_new, cache_ckv, cache_krope_t)


def _sample_ha_kernel(ol_ref, w_ref, o_ref):
    acc = None
    for hd in range(MLA_HEADS):
        t = _dot(ol_ref[hd].astype(BF), w_ref[hd])
        acc = t if acc is None else acc + t
    o_ref[...] = acc


def _sample_ha(o_lat_t, wuv_wide, l):
    b = o_lat_t.shape[1]
    return pl.pallas_call(
        _sample_ha_kernel,
        grid=(1,),
        in_specs=[_const_spec(o_lat_t.shape), _layer_spec(wuv_wide, l)],
        out_specs=pl.BlockSpec((b, MLA_HEADS * V_DIM), lambda i: (0, 0)),
        out_shape=jax.ShapeDtypeStruct((b, MLA_HEADS * V_DIM), F32),
        compiler_params=_params(("arbitrary",)),
        name="sample_ha",
    )(o_lat_t, wuv_wide)


def _gla_step_kernel(q_ref, k_ref, v_ref, lf_ref, sg_ref, g_ref, s0_ref, hc_ref, s_ref):
    dk = GLA_DK
    eye = lax.broadcasted_iota(jnp.int32, (dk, dk), 0) == lax.broadcasted_iota(jnp.int32, (dk, dk), 1)
    col = lambda rowv: jnp.sum(jnp.where(eye, jnp.broadcast_to(rowv, (dk, dk)), 0.0), axis=-1, keepdims=True)
    for bi in range(q_ref.shape[0]):
        rw = slice(bi, bi + 1)
        for hd in range(GLA_HEADS):
            ks = slice(hd * dk, (hd + 1) * dk)
            vs = slice(hd * GLA_DV, (hd + 1) * GLA_DV)
            s_new = jnp.exp(col(lf_ref[rw, ks])) * s0_ref[bi, hd] + col(k_ref[rw, ks]) * v_ref[rw, vs]
            s_ref[bi, hd] = s_new
            o = jnp.sum(col(q_ref[rw, ks]) * s_new, axis=0, keepdims=True)
            hc_ref[rw, vs] = _rms(o, g_ref[...]) * sg_ref[rw, vs]


def _gla_step(gq, gk, gv, lf, sg, g_gla, state_all, l):
    b = gq.shape[0]
    gb = GLA_STEP_BATCH
    assert b % gb == 0
    blk = lambda n: pl.BlockSpec((gb, n), lambda i: (i, 0))
    sshape = (GLA_HEADS, GLA_DK, GLA_DV)
    return pl.pallas_call(
        _gla_step_kernel,
        grid=(b // gb,),
        in_specs=[blk(GLA_KW), blk(GLA_KW), blk(GLA_VW), blk(GLA_KW), blk(GLA_VW), _layer_spec(g_gla, l),
                  pl.BlockSpec((None, gb) + sshape, lambda i: (l, i, 0, 0, 0))],
        out_specs=[blk(GLA_VW), pl.BlockSpec((gb,) + sshape, lambda i: (i, 0, 0, 0))],
        out_shape=[jax.ShapeDtypeStruct((b, GLA_VW), F32), jax.ShapeDtypeStruct((b,) + sshape, F32)],
        compiler_params=_params(("parallel",)),
        name="gla_step",
    )(gq, gk, gv, lf, sg, g_gla, state_all)


def kernel(x_prompt, x_sample, cache_ckv, cache_krope, state_gla, page_table, g_mix, w_in, g_q, w_uq, g_kv, w_uk, w_uv, g_v, b_v, w_s, b_s, w_a2, b_a, g_gla, w_pa, w_pb, w_pc, w_o, g_ffn, w_gu, w_down, g_final):
    batch, seq, _ = x_prompt.shape
    dec_b, dec_t, _ = x_sample.shape
    assert dec_t == 1 and seq % ROW_TILE == 0
    depth = w_in.shape[0]
    past = page_table.shape[1] * PAGE_SIZE
    ct_p, st_p = _rope_tables(jnp.arange(seq, dtype=F32))
    ct_s, st_s = _rope_tables(jnp.full((dec_b,), float(past), F32))
    gfin = g_final.reshape(1, -1).astype(F32)
    cache_krope_t = jnp.swapaxes(cache_krope, 2, 3)

    xp = x_prompt.reshape(batch * seq, D_MODEL)
    xs = x_sample.reshape(dec_b, D_MODEL)
    w = _prep_weights(g_mix, w_in, g_q, w_uq, g_kv, w_uk, w_uv, g_v, b_v, w_s, b_s, w_a2, b_a, g_gla,
                      w_pa, w_pb, w_pc, w_o, g_ffn, w_gu, w_down)
    ckv_p, kr_p, gla_p, ckv_s, kr_s, gla_s, gv_s = [], [], [], [], [], [], []
    for l in range(depth):
        last = l == depth - 1
        q, k, v, ckv, kr, hb, gq, gk, gv, lf, sg = _front(xp, ct_p, st_p, w, l, True, seq)
        ha = _attention(q, k, v, batch, seq)
        hc, s_c = _gla(gq, gk, gv, lf, sg, w['g_gla'], l, batch, seq)
        xp = _ffn(_merge(xp, ha, hb, hc, w, l, ROW_TILE), w, l, gfin, last, ROW_TILE)
        ckv_p.append(ckv.reshape(batch, seq, KV_RANK))
        kr_p.append(kr)
        gla_p.append(s_c)
        q, ql, ckv, kr, hb, vn, gq, gk, gv, lf, sg = _front(xs, ct_s, st_s, w, l, False, seq)
        qr = q.reshape(dec_b, MLA_HEADS, HEAD_PAD)[:, :, NOPE_DIM:NOPE_DIM + ROPE_DIM]
        o_lat = _decode(l, page_table, ql.reshape(dec_b, MLA_HEADS, KV_RANK), qr,
                        ckv.reshape(dec_b, 1, KV_RANK), kr.reshape(dec_b, 1, ROPE_DIM), cache_ckv, cache_krope_t)
        ha = _sample_ha(o_lat.transpose(1, 0, 2), w['wuv_wide'], l)
        hc, s_c = _gla_step(gq, gk, gv, lf, sg, w['g_gla'], state_gla, l)
        xs = _ffn(_merge(xs, ha, hb, hc, w, l, dec_b), w, l, gfin, last, dec_b)
        ckv_s.append(ckv.reshape(dec_b, 1, KV_RANK))
        kr_s.append(kr.reshape(dec_b, 1, ROPE_DIM))
        gla_s.append(s_c)
        gv_s.append(vn.reshape(dec_b, 1, GMLP_WIDTH))
    return (xp.reshape(batch, seq, D_MODEL), xs.reshape(dec_b, 1, D_MODEL), jnp.stack(ckv_p), jnp.swapaxes(jnp.stack(kr_p), 2, 3),
            jnp.stack(gla_p), jnp.stack(ckv_s), jnp.stack(kr_s), jnp.stack(gla_s), jnp.stack(gv_s))
```
